```python
import math
import jax, jax.numpy as jnp
from jax import lax
import numpy as np

D_MODEL = 1024
BATCH = 4
SEQ = 8192
DEPTH = 2

N_MIXERS = 2
HEAD_DIM = 64
N_MIX_HEADS = 12
W_MIX = N_MIX_HEADS * HEAD_DIM
N_MEM_HEADS = 4
W_MEM = N_MEM_HEADS * HEAD_DIM
N_MEM = 256
GMLP_CHUNK = 128
MOBA_BLOCK = 256
MOBA_TOPK = 3
MOBA_Q_BLOCK = 64
D_FF = int(math.ceil(8 * D_MODEL / 3 / 256) * 256)
ALPHA = (2.0 * DEPTH) ** 0.25
BETA = (8.0 * DEPTH) ** -0.25
LN_EPS = 1e-5

kernel_name = "hybrid_gmlp_moba_memxattn_deepnorm"


def layer_norm(x, g, b):
    xf = x.astype(jnp.float32)
    mu = jnp.mean(xf, axis=-1, keepdims=True)
    var = jnp.mean(jnp.square(xf - mu), axis=-1, keepdims=True)
    y = (xf - mu) * lax.rsqrt(var + LN_EPS)
    return (y * g.astype(jnp.float32) + b.astype(jnp.float32)).astype(x.dtype)


def gmlp_mixer(u, v, ln_g, ln_b, w_s, b_s):
    B, S, _ = u.shape
    u = jax.nn.gelu(u)
    v = layer_norm(jax.nn.gelu(v), ln_g, ln_b)
    vc = v.reshape(B, S // GMLP_CHUNK, GMLP_CHUNK, N_MIX_HEADS, HEAD_DIM)
    causal = jnp.tril(jnp.ones((GMLP_CHUNK, GMLP_CHUNK), w_s.dtype))
    w = w_s * causal[None]
    sv = jnp.einsum('gts,bcsgd->bctgd', w, vc) + b_s.T[None, None, :, :, None]
    return u * sv.reshape(B, S, W_MIX)


def moba_attention(q, k, v):
    B, S, H, dh = q.shape
    n_blk = -(-S // MOBA_BLOCK)
    Sp = n_blk * MOBA_BLOCK
    pad = ((0, 0), (0, Sp - S), (0, 0), (0, 0))
    qh = jnp.pad(q, pad).transpose(0, 2, 1, 3)
    kb = jnp.pad(k, pad).transpose(0, 2, 1, 3).reshape(B, H, n_blk, MOBA_BLOCK, dh)
    vb = jnp.pad(v, pad).transpose(0, 2, 1, 3).reshape(B, H, n_blk, MOBA_BLOCK, dh)
    k_mean = jnp.mean(kb.astype(jnp.float32), axis=3)
    topk = min(MOBA_TOPK, n_blk)
    scale = dh ** -0.5
    b_idx = jnp.arange(B)[:, None, None, None]
    h_idx = jnp.arange(H)[None, :, None, None]

    def one_query_block(c):
        start = c * MOBA_Q_BLOCK
        qc = lax.dynamic_slice_in_dim(qh, start, MOBA_Q_BLOCK, axis=2)
        j = start // MOBA_BLOCK
        blk_s = jnp.einsum('bhqd,bhnd->bhqn', qc.astype(jnp.float32), k_mean)
        blk_s = jnp.where(jnp.arange(n_blk) < j, blk_s, -jnp.inf)
        _, sel = lax.top_k(blk_s, topk)
        valid = jnp.arange(topk) < j
        k_sel = kb[b_idx, h_idx, sel]
        v_sel = vb[b_idx, h_idx, sel]
        s_sel = jnp.einsum('bhqd,bhqnkd->bhqnk', qc, k_sel).astype(jnp.float32) * scale
        s_sel = jnp.where(valid[:, None], s_sel, -jnp.inf)
        k_own = lax.dynamic_index_in_dim(kb, j, axis=2, keepdims=False)
        v_own = lax.dynamic_index_in_dim(vb, j, axis=2, keepdims=False)
        s_own = jnp.einsum('bhqd,bhkd->bhqk', qc, k_own).astype(jnp.float32) * scale
        q_pos = start + jnp.arange(MOBA_Q_BLOCK)
        k_pos = j * MOBA_BLOCK + jnp.arange(MOBA_BLOCK)
        s_own = jnp.where(k_pos[None, :] <= q_pos[:, None], s_own, -jnp.inf)
        s_all = jnp.concatenate(
            [s_sel.reshape(B, H, MOBA_Q_BLOCK, topk * MOBA_BLOCK), s_own], axis=-1)
        p = jax.nn.softmax(s_all, axis=-1)
        p_sel = p[..., :topk * MOBA_BLOCK].reshape(B, H, MOBA_Q_BLOCK, topk, MOBA_BLOCK)
        p_own = p[..., topk * MOBA_BLOCK:]
        out = (jnp.einsum('bhqnk,bhqnkd->bhqd', p_sel.astype(v.dtype), v_sel)
               + jnp.einsum('bhqk,bhkd->bhqd', p_own.astype(v.dtype), v_own))
        return out

    outs = lax.map(one_query_block, jnp.arange(Sp // MOBA_Q_BLOCK))
    out = outs.transpose(1, 2, 0, 3, 4).reshape(B, H, Sp, dh)
    return out.transpose(0, 2, 1, 3)[:, :S]


def memory_cross_attention(q_mem, mem, w_kv):
    B, S, _ = q_mem.shape
    q = q_mem.reshape(B, S, N_MEM_HEADS, HEAD_DIM)
    kv = mem @ w_kv
    k, v = jnp.split(kv, 2, axis=-1)
    k = k.reshape(B, -1, N_MEM_HEADS, HEAD_DIM)
    v = v.reshape(B, -1, N_MEM_HEADS, HEAD_DIM)
    s = jnp.einsum('bshd,bmhd->bhsm', q, k).astype(jnp.float32) * HEAD_DIM ** -0.5
    p = jax.nn.softmax(s, axis=-1).astype(v.dtype)
    return jnp.einsum('bhsm,bmhd->bshd', p, v).reshape(B, S, W_MEM)


def swiglu(x, w_in, w_out):
    g, u = jnp.split(x @ w_in, 2, axis=-1)
    return (jax.nn.silu(g) * u) @ w_out


def setup_inputs(seed: int = 0) -> dict:
    key = jax.random.key(seed)
    ks = jax.random.split(key, 16)
    n_a = (DEPTH + 1) // 2
    n_b = DEPTH // 2
    D = D_MODEL
    nrm = jax.random.normal
    f32 = jnp.float32
    return {
        "x": nrm(ks[0], (BATCH, SEQ, D), f32),
        "mem": nrm(ks[1], (BATCH, N_MEM, D), f32),
        "a_w_in": nrm(ks[2], (n_a, D, 2 * W_MIX + W_MEM), f32) * D ** -0.5,
        "a_ln_v_g": 1.0 + 0.02 * nrm(ks[3], (n_a, W_MIX), f32),
        "a_ln_v_b": 0.02 * nrm(ks[4], (n_a, W_MIX), f32),
        "a_w_s": nrm(ks[5], (n_a, N_MIX_HEADS, GMLP_CHUNK, GMLP_CHUNK), f32) * GMLP_CHUNK ** -0.5,
        "a_b_s": 1.0 + 0.1 * nrm(ks[6], (n_a, N_MIX_HEADS, GMLP_CHUNK), f32),
        "b_w_in": nrm(ks[7], (n_b, D, 3 * W_MIX + W_MEM), f32) * D ** -0.5,
        "w_mem_kv": nrm(ks[8], (DEPTH, D, 2 * W_MEM), f32) * D ** -0.5,
        "w_mix_out": nrm(ks[9], (DEPTH, W_MIX + W_MEM, D), f32) * (W_MIX + W_MEM) ** -0.5 * BETA,
        "ln_mix_g": 1.0 + 0.02 * nrm(ks[10], (DEPTH, D), f32),
        "ln_mix_b": 0.02 * nrm(ks[11], (DEPTH, D), f32),
        "w_ffn_in": nrm(ks[12], (DEPTH, D, 2 * D_FF), f32) * D ** -0.5,
        "w_ffn_out": nrm(ks[13], (DEPTH, D_FF, D), f32) * D_FF ** -0.5 * BETA,
        "ln_ffn_g": 1.0 + 0.02 * nrm(ks[14], (DEPTH, D), f32),
        "ln_ffn_b": 0.02 * nrm(ks[15], (DEPTH, D), f32),
    }


def reference(x, mem, a_w_in, a_ln_v_g, a_ln_v_b, a_w_s, a_b_s, b_w_in, w_mem_kv,
              w_mix_out, ln_mix_g, ln_mix_b, w_ffn_in, w_ffn_out, ln_ffn_g, ln_ffn_b):
    B, S, _ = x.shape
    for i in range(DEPTH):
        li = i // N_MIXERS
        if i % N_MIXERS == 0:
            h = x @ a_w_in[li]
            u, v, q_mem = jnp.split(h, [W_MIX, 2 * W_MIX], axis=-1)
            mix = gmlp_mixer(u, v, a_ln_v_g[li], a_ln_v_b[li], a_w_s[li], a_b_s[li])
        else:
            h = x @ b_w_in[li]
            q, k, v, q_mem = jnp.split(h, [W_MIX, 2 * W_MIX, 3 * W_MIX], axis=-1)
            shp = (B, S, N_MIX_HEADS, HEAD_DIM)
            mix = moba_attention(q.reshape(shp), k.reshape(shp), v.reshape(shp)).reshape(B, S, W_MIX)
        mem_out = memory_cross_attention(q_mem, mem, w_mem_kv[i])
        sub = jnp.concatenate([mix, mem_out], axis=-1) @ w_mix_out[i]
        x = layer_norm(ALPHA * x + sub, ln_mix_g[i], ln_mix_b[i])
        x = layer_norm(ALPHA * x + swiglu(x, w_ffn_in[i], w_ffn_out[i]), ln_ffn_g[i], ln_ffn_b[i])
    return x
```

```python
import functools

import jax
import jax.numpy as jnp
from jax import lax
from jax.experimental import pallas as pl
from jax.experimental.pallas import tpu as pltpu

HEAD_DIM = 64
N_MIX_HEADS = 12
W_MIX = N_MIX_HEADS * HEAD_DIM
N_MEM_HEADS = 4
W_MEM = N_MEM_HEADS * HEAD_DIM
GMLP_CHUNK = 128
MOBA_BLOCK = 256
MOBA_TOPK = 3
DEPTH = 2
ALPHA = (2.0 * DEPTH) ** 0.25
LN_EPS = 1e-5
ATTN_SCALE = HEAD_DIM ** -0.5

LANES = 128
HEADS_PER_LANE_TILE = LANES // HEAD_DIM
TOKEN_TILE = 512
FFN_CHUNK = 256
VMEM_LIMIT_BYTES = 56 * 1024 * 1024

BF16 = jnp.bfloat16
F32 = jnp.float32

_NT = (((1,), (1,)), ((), ()))


def _layer_norm(y, g, b):
    mu = jnp.mean(y, axis=-1, keepdims=True)
    d = y - mu
    var = jnp.mean(d * d, axis=-1, keepdims=True)
    return d * lax.rsqrt(var + LN_EPS) * g + b


def _head_lane_mask(e):
    lane = lax.broadcasted_iota(jnp.int32, (1, LANES), 1)
    return (lane >= e * HEAD_DIM) & (lane < (e + 1) * HEAD_DIM)


def _mem_cross_attention(qm, kv_ref):
    qb = (qm * ATTN_SCALE).astype(BF16)
    v_all = kv_ref[:, W_MEM:]
    lane_v = lax.broadcasted_iota(jnp.int32, (1, W_MEM), 1)
    out = None
    for hm in range(N_MEM_HEADS):
        tile, e = divmod(hm, HEADS_PER_LANE_TILE)
        qp = qb[:, tile * LANES:(tile + 1) * LANES]
        qz = jnp.where(_head_lane_mask(e), qp, jnp.zeros_like(qp))
        kp = kv_ref[:, tile * LANES:(tile + 1) * LANES]
        s = lax.dot_general(qz, kp, _NT, preferred_element_type=F32)
        m = jnp.max(s, axis=-1, keepdims=True)
        p = jnp.exp(s - m)
        l = jnp.sum(p, axis=-1, keepdims=True)
        pn = (p * (1.0 / l)).astype(BF16)
        vz = jnp.where((lane_v >= hm * HEAD_DIM) & (lane_v < (hm + 1) * HEAD_DIM),
                       v_all, jnp.zeros_like(v_all))
        o = jnp.dot(pn, vz, preferred_element_type=F32)
        out = o if out is None else out + o
    return out


def _mem_kv_kernel(mem_ref, w_ref, o_ref):
    o_ref[0] = jnp.dot(mem_ref[...].astype(BF16), w_ref[0].astype(BF16),
                       preferred_element_type=F32).astype(BF16)


def _mem_kv(mem2d, w_mem_kv):
    rows, d = mem2d.shape
    depth, _, n = w_mem_kv.shape
    return pl.pallas_call(
        _mem_kv_kernel,
        grid=(depth,),
        in_specs=[pl.BlockSpec((rows, d), lambda i: (0, 0)),
                  pl.BlockSpec((1, d, n), lambda i: (i, 0, 0))],
        out_specs=pl.BlockSpec((1, rows, n), lambda i: (i, 0, 0)),
        out_shape=jax.ShapeDtypeStruct((depth, rows, n), BF16),
        compiler_params=pltpu.CompilerParams(dimension_semantics=("parallel",),
                                             vmem_limit_bytes=VMEM_LIMIT_BYTES),
        name="mem_kv",
    )(mem2d, w_mem_kv)


def _gmlp_front_kernel(x_ref, w_ref, lng_ref, lnb_ref, ws_ref, bs_ref, kv_ref,
                       mix_ref, mem_ref):
    t = x_ref.shape[1]
    xb = x_ref[0].astype(BF16)
    u = jax.nn.gelu(jnp.dot(xb, w_ref[:, :W_MIX], preferred_element_type=F32))
    gv = jax.nn.gelu(jnp.dot(xb, w_ref[:, W_MIX:2 * W_MIX], preferred_element_type=F32))
    vb = _layer_norm(gv, lng_ref[...], lnb_ref[...]).astype(BF16)
    qm = jnp.dot(xb, w_ref[:, 2 * W_MIX:], preferred_element_type=F32)

    row = lax.broadcasted_iota(jnp.int32, (GMLP_CHUNK, GMLP_CHUNK), 0)
    col = lax.broadcasted_iota(jnp.int32, (GMLP_CHUNK, GMLP_CHUNK), 1)
    causal = col <= row
    m0 = _head_lane_mask(0)
    m1 = _head_lane_mask(1)
    for p in range(N_MIX_HEADS // HEADS_PER_LANE_TILE):
        w0 = ws_ref[2 * p]
        w1 = ws_ref[2 * p + 1]
        wcat = jnp.concatenate([jnp.where(causal, w0, jnp.zeros_like(w0)),
                                jnp.where(causal, w1, jnp.zeros_like(w1))], axis=1)
        bias = bs_ref[:, p * LANES:(p + 1) * LANES]
        for c in range(t // GMLP_CHUNK):
            rows = slice(c * GMLP_CHUNK, (c + 1) * GMLP_CHUNK)
            vp = vb[rows, p * LANES:(p + 1) * LANES]
            zero = jnp.zeros_like(vp)
            vbd = jnp.concatenate([jnp.where(m0, vp, zero), jnp.where(m1, vp, zero)], axis=0)
            sv = jnp.dot(wcat, vbd, preferred_element_type=F32) + bias
            mix_ref[0, rows, p * LANES:(p + 1) * LANES] = (
                u[rows, p * LANES:(p + 1) * LANES] * sv).astype(BF16)

    mem_ref[0] = _mem_cross_attention(qm, kv_ref.at[0]).astype(BF16)


def _gmlp_front(x, w_in, ln_g, ln_b, w_s, b_full, kv):
    b, s, d = x.shape
    t = TOKEN_TILE
    n_in = w_in.shape[1]
    return pl.pallas_call(
        _gmlp_front_kernel,
        grid=(b, s // t),
        in_specs=[
            pl.BlockSpec((1, t, d), lambda bi, i: (bi, i, 0)),
            pl.BlockSpec((d, n_in), lambda bi, i: (0, 0)),
            pl.BlockSpec((1, W_MIX), lambda bi, i: (0, 0)),
            pl.BlockSpec((1, W_MIX), lambda bi, i: (0, 0)),
            pl.BlockSpec((N_MIX_HEADS, GMLP_CHUNK, GMLP_CHUNK), lambda bi, i: (0, 0, 0)),
            pl.BlockSpec((GMLP_CHUNK, W_MIX), lambda bi, i: (0, 0)),
            pl.BlockSpec((1, kv.shape[1], kv.shape[2]), lambda bi, i: (bi, 0, 0)),
        ],
        out_specs=[pl.BlockSpec((1, t, W_MIX), lambda bi, i: (bi, i, 0)),
                   pl.BlockSpec((1, t, W_MEM), lambda bi, i: (bi, i, 0))],
        out_shape=[jax.ShapeDtypeStruct((b, s, W_MIX), BF16),
                   jax.ShapeDtypeStruct((b, s, W_MEM), BF16)],
        compiler_params=pltpu.CompilerParams(dimension_semantics=("parallel", "parallel"),
                                             vmem_limit_bytes=VMEM_LIMIT_BYTES),
        name="gmlp_front",
    )(x, w_in, ln_g, ln_b, w_s, b_full, kv)


def _moba_front_kernel(x_ref, wqkm_ref, wvt_ref, kv_ref,
                       q_ref, k_ref, vt_ref, km_ref, mem_ref):
    t = x_ref.shape[1]
    xb = x_ref[0].astype(BF16)
    q = jnp.dot(xb, wqkm_ref[:, :W_MIX], preferred_element_type=F32)
    q_ref[0] = (q * ATTN_SCALE).astype(BF16)
    k = jnp.dot(xb, wqkm_ref[:, W_MIX:2 * W_MIX], preferred_element_type=F32)
    k_ref[0] = k.astype(BF16)
    for c in range(t // MOBA_BLOCK):
        km_ref[0, 0, c:c + 1, :] = jnp.mean(k[c * MOBA_BLOCK:(c + 1) * MOBA_BLOCK], axis=0,
                                            keepdims=True)
    vt = lax.dot_general(wvt_ref[...], xb, _NT, preferred_element_type=F32)
    for c in range(t // MOBA_BLOCK):
        vt_ref[0, c] = vt[:, c * MOBA_BLOCK:(c + 1) * MOBA_BLOCK].astype(BF16)
    qm = jnp.dot(xb, wqkm_ref[:, 2 * W_MIX:], preferred_element_type=F32)
    mem_ref[0] = _mem_cross_attention(qm, kv_ref.at[0]).astype(BF16)


def _moba_front(x, w_qkm, w_vt, kv):
    b, s, d = x.shape
    t = TOKEN_TILE
    bpt = t // MOBA_BLOCK
    n_blk = s // MOBA_BLOCK
    return pl.pallas_call(
        _moba_front_kernel,
        grid=(b, s // t),
        in_specs=[
            pl.BlockSpec((1, t, d), lambda bi, i: (bi, i, 0)),
            pl.BlockSpec(w_qkm.shape, lambda bi, i: (0, 0)),
            pl.BlockSpec(w_vt.shape, lambda bi, i: (0, 0)),
            pl.BlockSpec((1, kv.shape[1], kv.shape[2]), lambda bi, i: (bi, 0, 0)),
        ],
        out_specs=[
            pl.BlockSpec((1, t, W_MIX), lambda bi, i: (bi, i, 0)),
            pl.BlockSpec((1, t, W_MIX), lambda bi, i: (bi, i, 0)),
            pl.BlockSpec((1, bpt, W_MIX, MOBA_BLOCK), lambda bi, i: (bi, i, 0, 0)),
            pl.BlockSpec((1, 1, bpt, W_MIX), lambda bi, i: (bi, i, 0, 0)),
            pl.BlockSpec((1, t, W_MEM), lambda bi, i: (bi, i, 0)),
        ],
        out_shape=[
            jax.ShapeDtypeStruct((b, s, W_MIX), BF16),
            jax.ShapeDtypeStruct((b, s, W_MIX), BF16),
            jax.ShapeDtypeStruct((b, n_blk, W_MIX, MOBA_BLOCK), BF16),
            jax.ShapeDtypeStruct((b, s // t, bpt, W_MIX), F32),
            jax.ShapeDtypeStruct((b, s, W_MEM), BF16),
        ],
        compiler_params=pltpu.CompilerParams(dimension_semantics=("parallel", "parallel"),
                                             vmem_limit_bytes=VMEM_LIMIT_BYTES),
        name="moba_front",
    )(x, w_qkm, w_vt, kv)


def _select_blocks(bs_t, n_past):
    n_blk = bs_t.shape[0]
    nidx = lax.broadcasted_iota(jnp.int32, bs_t.shape, 0)
    rem = nidx < n_past
    sel = jnp.zeros(bs_t.shape, jnp.bool_)
    for _ in range(MOBA_TOPK):
        ms = jnp.where(rem, bs_t, -jnp.inf)
        mx = jnp.max(ms, axis=0, keepdims=True)
        is_max = rem & (ms == mx)
        first = jnp.min(jnp.where(is_max, nidx, n_blk), axis=0, keepdims=True)
        pick = is_max & (nidx == first)
        sel = sel | pick
        rem = rem & jnp.logical_not(pick)
    return sel


def _moba_attn_kernel(q_ref, k_ref, vt_ref, km_ref, o_ref, bias_ref, m_ref, l_ref, acc_ref):
    j = pl.program_id(2)
    qp = q_ref[0]
    kmp = km_ref[0].astype(BF16)
    krow = lax.broadcasted_iota(jnp.int32, (MOBA_BLOCK, MOBA_BLOCK), 0)
    qcol = lax.broadcasted_iota(jnp.int32, (MOBA_BLOCK, MOBA_BLOCK), 1)
    causal = krow <= qcol
    qz = []
    for e in range(HEADS_PER_LANE_TILE):
        qz_e = jnp.where(_head_lane_mask(e), qp, jnp.zeros_like(qp))
        qz.append(qz_e)
        bs_t = lax.dot_general(kmp, qz_e, _NT, preferred_element_type=F32)
        bias_ref[e] = jnp.where(_select_blocks(bs_t, j), 0.0, -jnp.inf).astype(F32)
        s_t = lax.dot_general(k_ref[0, j], qz_e, _NT, preferred_element_type=F32)
        s_t = jnp.where(causal, s_t, -jnp.inf)
        m = jnp.max(s_t, axis=0, keepdims=True)
        p = jnp.exp(s_t - m)
        m_ref[e] = m
        l_ref[e] = jnp.sum(p, axis=0, keepdims=True)
        acc_ref[e] = jnp.dot(vt_ref[0, j, e * HEAD_DIM:(e + 1) * HEAD_DIM, :], p.astype(BF16),
                             preferred_element_type=F32)

    def past_block(n, carry):
        kn = k_ref[0, n]
        for e in range(HEADS_PER_LANE_TILE):
            s_t = lax.dot_general(kn, qz[e], _NT, preferred_element_type=F32)
            s_t = s_t + bias_ref[e, pl.ds(n, 1), :]
            m_old = m_ref[e]
            m_new = jnp.maximum(m_old, jnp.max(s_t, axis=0, keepdims=True))
            a = jnp.exp(m_old - m_new)
            p = jnp.exp(s_t - m_new)
            m_ref[e] = m_new
            l_ref[e] = a * l_ref[e] + jnp.sum(p, axis=0, keepdims=True)
            pv = jnp.dot(vt_ref[0, n, e * HEAD_DIM:(e + 1) * HEAD_DIM, :], p.astype(BF16),
                         preferred_element_type=F32)
            acc_ref[e] = a * acc_ref[e] + pv
        return carry

    lax.fori_loop(0, j, past_block, 0)

    o_t = jnp.concatenate([acc_ref[e] / l_ref[e] for e in range(HEADS_PER_LANE_TILE)], axis=0)
    o_ref[0] = o_t.T.astype(BF16)


def _moba_attn(q, k4, vt, km):
    b, s, _ = q.shape
    n_blk = s // MOBA_BLOCK
    n_tiles = W_MIX // LANES
    return pl.pallas_call(
        _moba_attn_kernel,
        grid=(b, n_tiles, n_blk),
        in_specs=[
            pl.BlockSpec((1, MOBA_BLOCK, LANES), lambda bi, p, j: (bi, j, p)),
            pl.BlockSpec((1, n_blk, MOBA_BLOCK, LANES), lambda bi, p, j: (bi, 0, 0, p)),
            pl.BlockSpec((1, n_blk, LANES, MOBA_BLOCK), lambda bi, p, j: (bi, 0, p, 0)),
            pl.BlockSpec((1, n_blk, LANES), lambda bi, p, j: (bi, 0, p)),
        ],
        out_specs=pl.BlockSpec((1, MOBA_BLOCK, LANES), lambda bi, p, j: (bi, j, p)),
        out_shape=jax.ShapeDtypeStruct((b, s, W_MIX), BF16),
        scratch_shapes=[
            pltpu.VMEM((HEADS_PER_LANE_TILE, n_blk, MOBA_BLOCK), F32),
            pltpu.VMEM((HEADS_PER_LANE_TILE, 1, MOBA_BLOCK), F32),
            pltpu.VMEM((HEADS_PER_LANE_TILE, 1, MOBA_BLOCK), F32),
            pltpu.VMEM((HEADS_PER_LANE_TILE, HEAD_DIM, MOBA_BLOCK), F32),
        ],
        compiler_params=pltpu.CompilerParams(
            dimension_semantics=("parallel", "parallel", "parallel"),
            vmem_limit_bytes=VMEM_LIMIT_BYTES),
        name="moba_attn",
    )(q, k4, vt, km)


def _post_mixer_kernel(x_ref, mix_ref, mem_ref, woa_ref, wob_ref, g1_ref, b1_ref,
                       wg_ref, wu_ref, wd_ref, g2_ref, b2_ref, o_ref, act_ref):
    sub = (jnp.dot(mix_ref[0], woa_ref[...], preferred_element_type=F32)
           + jnp.dot(mem_ref[0], wob_ref[...], preferred_element_type=F32))
    x1 = _layer_norm(ALPHA * x_ref[0] + sub, g1_ref[...], b1_ref[...])
    x1b = x1.astype(BF16)
    d_ff = wg_ref.shape[1]
    for c in range(d_ff // FFN_CHUNK):
        cols = slice(c * FFN_CHUNK, (c + 1) * FFN_CHUNK)
        g = jnp.dot(x1b, wg_ref[:, cols], preferred_element_type=F32)
        u = jnp.dot(x1b, wu_ref[:, cols], preferred_element_type=F32)
        act_ref[:, cols] = (jax.nn.silu(g) * u).astype(BF16)
    y = jnp.dot(act_ref[...], wd_ref[...], preferred_element_type=F32)
    o_ref[0] = _layer_norm(ALPHA * x1 + y, g2_ref[...], b2_ref[...])


def _post_mixer(x, mix, mem_out, wo_a, wo_b, g1, b1, wg, wu, wd, g2, b2):
    b, s, d = x.shape
    t = TOKEN_TILE
    d_ff = wg.shape[1]
    const = lambda bi, i: (0, 0)
    single = pl.Buffered(1)

    def wspec(shape):
        return pl.BlockSpec(shape, const, pipeline_mode=single)

    return pl.pallas_call(
        _post_mixer_kernel,
        grid=(b, s // t),
        in_specs=[
            pl.BlockSpec((1, t, d), lambda bi, i: (bi, i, 0)),
            pl.BlockSpec((1, t, W_MIX), lambda bi, i: (bi, i, 0)),
            pl.BlockSpec((1, t, W_MEM), lambda bi, i: (bi, i, 0)),
            wspec(wo_a.shape), wspec(wo_b.shape), wspec(g1.shape), wspec(b1.shape),
            wspec(wg.shape), wspec(wu.shape), wspec(wd.shape), wspec(g2.shape), wspec(b2.shape),
        ],
        out_specs=pl.BlockSpec((1, t, d), lambda bi, i: (bi, i, 0)),
        out_shape=jax.ShapeDtypeStruct((b, s, d), F32),
        scratch_shapes=[pltpu.VMEM((t, d_ff), BF16)],
        compiler_params=pltpu.CompilerParams(dimension_semantics=("parallel", "parallel"),
                                             vmem_limit_bytes=VMEM_LIMIT_BYTES),
        name="post_mixer",
    )(x, mix, mem_out, wo_a, wo_b, g1, b1, wg, wu, wd, g2, b2)


def kernel(x, mem, a_w_in, a_ln_v_g, a_ln_v_b, a_w_s, a_b_s, b_w_in, w_mem_kv, w_mix_out,
           ln_mix_g, ln_mix_b, w_ffn_in, w_ffn_out, ln_ffn_g, ln_ffn_b):
    b, s, d = x.shape
    n_mem = mem.shape[1]
    d_ff = w_ffn_out.shape[1]
    assert s % TOKEN_TILE == 0 and TOKEN_TILE % MOBA_BLOCK == 0 and d_ff % FFN_CHUNK == 0
    n_blk = s // MOBA_BLOCK

    kv = _mem_kv(mem.reshape(b * n_mem, d), w_mem_kv).reshape(DEPTH, b, n_mem, 2 * W_MEM)

    def post(i, xin, mix, mem_out):
        wo = w_mix_out[i].astype(BF16)
        wi = w_ffn_in[i].astype(BF16)
        row = lambda v: v.reshape(1, -1)
        return _post_mixer(xin, mix, mem_out, wo[:W_MIX], wo[W_MIX:],
                           row(ln_mix_g[i]), row(ln_mix_b[i]),
                           wi[:, :d_ff], wi[:, d_ff:], w_ffn_out[i].astype(BF16),
                           row(ln_ffn_g[i]), row(ln_ffn_b[i]))

    b_full = jnp.repeat(a_b_s[0].T, HEAD_DIM, axis=1)
    mix, mem_out = _gmlp_front(x, a_w_in[0].astype(BF16), a_ln_v_g[0].reshape(1, -1),
                               a_ln_v_b[0].reshape(1, -1), a_w_s[0].astype(BF16), b_full, kv[0])
    x = post(0, x, mix, mem_out)

    w1 = b_w_in[0].astype(BF16)
    w_qkm = jnp.concatenate([w1[:, :2 * W_MIX], w1[:, 3 * W_MIX:]], axis=1)
    w_vt = w1[:, 2 * W_MIX:3 * W_MIX].T
    q, k, vt, km, mem_out = _moba_front(x, w_qkm, w_vt, kv[1])
    mix = _moba_attn(q, k.reshape(b, n_blk, MOBA_BLOCK, W_MIX), vt, km.reshape(b, n_blk, W_MIX))
    return post(1, x, mix, mem_out)
```

```python
import functools

import jax
import jax.numpy as jnp
from jax import lax
from jax.experimental import pallas as pl
from jax.experimental.pallas import tpu as pltpu

HEAD_DIM = 64
N_MIX_HEADS = 12
W_MIX = N_MIX_HEADS * HEAD_DIM
N_MEM_HEADS = 4
W_MEM = N_MEM_HEADS * HEAD_DIM
GMLP_CHUNK = 128
MOBA_BLOCK = 256
MOBA_TOPK = 3
DEPTH = 2
ALPHA = (2.0 * DEPTH) ** 0.25
LN_EPS = 1e-5
ATTN_SCALE = HEAD_DIM ** -0.5
LOG2_E = 1.4426950408889634
BF16_SUBLANES = 16
V_AUG_ROWS = HEAD_DIM + BF16_SUBLANES

LANES = 128
HEADS_PER_LANE_TILE = LANES // HEAD_DIM
TOKEN_TILE = 512
FFN_CHUNK = 256
VMEM_LIMIT_BYTES = 56 * 1024 * 1024

BF16 = jnp.bfloat16
F32 = jnp.float32

_NT = (((1,), (1,)), ((), ()))


def _layer_norm(y, g, b):
    mu = jnp.mean(y, axis=-1, keepdims=True)
    d = y - mu
    var = jnp.mean(d * d, axis=-1, keepdims=True)
    return d * lax.rsqrt(var + LN_EPS) * g + b


def _head_lane_mask(e):
    lane = lax.broadcasted_iota(jnp.int32, (1, LANES), 1)
    return (lane >= e * HEAD_DIM) & (lane < (e + 1) * HEAD_DIM)


def _mem_cross_attention(qm, kv_ref):
    qb = (qm * ATTN_SCALE).astype(BF16)
    v_all = kv_ref[:, W_MEM:]
    lane_v = lax.broadcasted_iota(jnp.int32, (1, W_MEM), 1)
    out = None
    for hm in range(N_MEM_HEADS):
        tile, e = divmod(hm, HEADS_PER_LANE_TILE)
        qp = qb[:, tile * LANES:(tile + 1) * LANES]
        qz = jnp.where(_head_lane_mask(e), qp, jnp.zeros_like(qp))
        kp = kv_ref[:, tile * LANES:(tile + 1) * LANES]
        s = lax.dot_general(qz, kp, _NT, preferred_element_type=F32)
        m = jnp.max(s, axis=-1, keepdims=True)
        p = jnp.exp(s - m)
        l = jnp.sum(p, axis=-1, keepdims=True)
        pn = (p * (1.0 / l)).astype(BF16)
        vz = jnp.where((lane_v >= hm * HEAD_DIM) & (lane_v < (hm + 1) * HEAD_DIM),
                       v_all, jnp.zeros_like(v_all))
        o = jnp.dot(pn, vz, preferred_element_type=F32)
        out = o if out is None else out + o
    return out


def _mem_kv_kernel(mem_ref, w_ref, o_ref):
    o_ref[0] = jnp.dot(mem_ref[...].astype(BF16), w_ref[0].astype(BF16),
                       preferred_element_type=F32).astype(BF16)


def _mem_kv(mem2d, w_mem_kv):
    rows, d = mem2d.shape
    depth, _, n = w_mem_kv.shape
    return pl.pallas_call(
        _mem_kv_kernel,
        grid=(depth,),
        in_specs=[pl.BlockSpec((rows, d), lambda i: (0, 0)),
                  pl.BlockSpec((1, d, n), lambda i: (i, 0, 0))],
        out_specs=pl.BlockSpec((1, rows, n), lambda i: (i, 0, 0)),
        out_shape=jax.ShapeDtypeStruct((depth, rows, n), BF16),
        compiler_params=pltpu.CompilerParams(dimension_semantics=("parallel",),
                                             vmem_limit_bytes=VMEM_LIMIT_BYTES),
        name="mem_kv",
    )(mem2d, w_mem_kv)


def _gmlp_front_kernel(x_ref, w_ref, lng_ref, lnb_ref, ws_ref, bs_ref, kv_ref,
                       mix_ref, mem_ref):
    t = x_ref.shape[1]
    xb = x_ref[0].astype(BF16)
    u = jax.nn.gelu(jnp.dot(xb, w_ref[:, :W_MIX], preferred_element_type=F32))
    gv = jax.nn.gelu(jnp.dot(xb, w_ref[:, W_MIX:2 * W_MIX], preferred_element_type=F32))
    vb = _layer_norm(gv, lng_ref[...], lnb_ref[...]).astype(BF16)
    qm = jnp.dot(xb, w_ref[:, 2 * W_MIX:], preferred_element_type=F32)

    row = lax.broadcasted_iota(jnp.int32, (GMLP_CHUNK, GMLP_CHUNK), 0)
    col = lax.broadcasted_iota(jnp.int32, (GMLP_CHUNK, GMLP_CHUNK), 1)
    causal = col <= row
    m0 = _head_lane_mask(0)
    m1 = _head_lane_mask(1)
    for p in range(N_MIX_HEADS // HEADS_PER_LANE_TILE):
        w0 = ws_ref[2 * p]
        w1 = ws_ref[2 * p + 1]
        wcat = jnp.concatenate([jnp.where(causal, w0, jnp.zeros_like(w0)),
                                jnp.where(causal, w1, jnp.zeros_like(w1))], axis=1)
        bias = bs_ref[:, p * LANES:(p + 1) * LANES]
        for c in range(t // GMLP_CHUNK):
            rows = slice(c * GMLP_CHUNK, (c + 1) * GMLP_CHUNK)
            vp = vb[rows, p * LANES:(p + 1) * LANES]
            zero = jnp.zeros_like(vp)
            vbd = jnp.concatenate([jnp.where(m0, vp, zero), jnp.where(m1, vp, zero)], axis=0)
            sv = jnp.dot(wcat, vbd, preferred_element_type=F32) + bias
            mix_ref[0, rows, p * LANES:(p + 1) * LANES] = (
                u[rows, p * LANES:(p + 1) * LANES] * sv).astype(BF16)

    mem_ref[0] = _mem_cross_attention(qm, kv_ref.at[0]).astype(BF16)


def _gmlp_front(x, w_in, ln_g, ln_b, w_s, b_full, kv):
    b, s, d = x.shape
    t = TOKEN_TILE
    n_in = w_in.shape[1]
    return pl.pallas_call(
        _gmlp_front_kernel,
        grid=(b, s // t),
        in_specs=[
            pl.BlockSpec((1, t, d), lambda bi, i: (bi, i, 0)),
            pl.BlockSpec((d, n_in), lambda bi, i: (0, 0)),
            pl.BlockSpec((1, W_MIX), lambda bi, i: (0, 0)),
            pl.BlockSpec((1, W_MIX), lambda bi, i: (0, 0)),
            pl.BlockSpec((N_MIX_HEADS, GMLP_CHUNK, GMLP_CHUNK), lambda bi, i: (0, 0, 0)),
            pl.BlockSpec((GMLP_CHUNK, W_MIX), lambda bi, i: (0, 0)),
            pl.BlockSpec((1, kv.shape[1], kv.shape[2]), lambda bi, i: (bi, 0, 0)),
        ],
        out_specs=[pl.BlockSpec((1, t, W_MIX), lambda bi, i: (bi, i, 0)),
                   pl.BlockSpec((1, t, W_MEM), lambda bi, i: (bi, i, 0))],
        out_shape=[jax.ShapeDtypeStruct((b, s, W_MIX), BF16),
                   jax.ShapeDtypeStruct((b, s, W_MEM), BF16)],
        compiler_params=pltpu.CompilerParams(dimension_semantics=("parallel", "parallel"),
                                             vmem_limit_bytes=VMEM_LIMIT_BYTES),
        name="gmlp_front",
    )(x, w_in, ln_g, ln_b, w_s, b_full, kv)


def _moba_front_kernel(x_ref, wqkm_ref, wvt_ref, kv_ref,
                       q_ref, k_ref, vt_ref, km_ref, mem_ref):
    t = x_ref.shape[1]
    xb = x_ref[0].astype(BF16)
    q = jnp.dot(xb, wqkm_ref[:, :W_MIX], preferred_element_type=F32)
    q_ref[0] = (q * (ATTN_SCALE * LOG2_E)).astype(BF16)
    k = jnp.dot(xb, wqkm_ref[:, W_MIX:2 * W_MIX], preferred_element_type=F32)
    k_ref[0] = k.astype(BF16)
    for c in range(t // MOBA_BLOCK):
        km_ref[0, 0, c:c + 1, :] = jnp.mean(k[c * MOBA_BLOCK:(c + 1) * MOBA_BLOCK], axis=0,
                                            keepdims=True)
    vt = lax.dot_general(wvt_ref[...], xb, _NT, preferred_element_type=F32)
    ones = jnp.ones((V_AUG_ROWS - HEAD_DIM, MOBA_BLOCK), BF16)
    for c in range(t // MOBA_BLOCK):
        for h in range(N_MIX_HEADS):
            vt_ref[0, c, h, :HEAD_DIM, :] = vt[h * HEAD_DIM:(h + 1) * HEAD_DIM,
                                               c * MOBA_BLOCK:(c + 1) * MOBA_BLOCK].astype(BF16)
            vt_ref[0, c, h, HEAD_DIM:, :] = ones
    qm = jnp.dot(xb, wqkm_ref[:, 2 * W_MIX:], preferred_element_type=F32)
    mem_ref[0] = _mem_cross_attention(qm, kv_ref.at[0]).astype(BF16)


def _moba_front(x, w_qkm, w_vt, kv):
    b, s, d = x.shape
    t = TOKEN_TILE
    bpt = t // MOBA_BLOCK
    n_blk = s // MOBA_BLOCK
    return pl.pallas_call(
        _moba_front_kernel,
        grid=(b, s // t),
        in_specs=[
            pl.BlockSpec((1, t, d), lambda bi, i: (bi, i, 0)),
            pl.BlockSpec(w_qkm.shape, lambda bi, i: (0, 0)),
            pl.BlockSpec(w_vt.shape, lambda bi, i: (0, 0)),
            pl.BlockSpec((1, kv.shape[1], kv.shape[2]), lambda bi, i: (bi, 0, 0)),
        ],
        out_specs=[
            pl.BlockSpec((1, t, W_MIX), lambda bi, i: (bi, i, 0)),
            pl.BlockSpec((1, t, W_MIX), lambda bi, i: (bi, i, 0)),
            pl.BlockSpec((1, bpt, N_MIX_HEADS, V_AUG_ROWS, MOBA_BLOCK),
                         lambda bi, i: (bi, i, 0, 0, 0)),
            pl.BlockSpec((1, 1, bpt, W_MIX), lambda bi, i: (bi, i, 0, 0)),
            pl.BlockSpec((1, t, W_MEM), lambda bi, i: (bi, i, 0)),
        ],
        out_shape=[
            jax.ShapeDtypeStruct((b, s, W_MIX), BF16),
            jax.ShapeDtypeStruct((b, s, W_MIX), BF16),
            jax.ShapeDtypeStruct((b, n_blk, N_MIX_HEADS, V_AUG_ROWS, MOBA_BLOCK), BF16),
            jax.ShapeDtypeStruct((b, s // t, bpt, W_MIX), F32),
            jax.ShapeDtypeStruct((b, s, W_MEM), BF16),
        ],
        compiler_params=pltpu.CompilerParams(dimension_semantics=("parallel", "parallel"),
                                             vmem_limit_bytes=VMEM_LIMIT_BYTES),
        name="moba_front",
    )(x, w_qkm, w_vt, kv)


def _select_blocks(bs_t, n_past):
    n_blk = bs_t.shape[0]
    nidx = lax.broadcasted_iota(jnp.int32, bs_t.shape, 0)
    rem = nidx < n_past
    sel = jnp.zeros(bs_t.shape, jnp.bool_)
    for _ in range(MOBA_TOPK):
        ms = jnp.where(rem, bs_t, -jnp.inf)
        mx = jnp.max(ms, axis=0, keepdims=True)
        is_max = rem & (ms == mx)
        first = jnp.min(jnp.where(is_max, nidx, n_blk), axis=0, keepdims=True)
        pick = is_max & (nidx == first)
        sel = sel | pick
        rem = rem & jnp.logical_not(pick)
    return sel


def _moba_attn_kernel(q_ref, k_ref, vt_ref, km_ref, o_ref, bias_ref, s_ref, p_ref):
    j = pl.program_id(2)
    nq = MOBA_BLOCK
    heads = range(HEADS_PER_LANE_TILE)
    qp = q_ref[0]
    zero = jnp.zeros_like(qp)
    qz = jnp.concatenate([jnp.where(_head_lane_mask(e), qp, zero) for e in heads], axis=0)
    bs_t = lax.dot_general(km_ref[0].astype(BF16), qz, _NT, preferred_element_type=F32)
    bias_ref[...] = jnp.where(_select_blocks(bs_t, j), 0.0, -jnp.inf).astype(F32)

    def head_cols(a, e):
        return a[:, e * nq:(e + 1) * nq]

    width = HEADS_PER_LANE_TILE * nq
    own_rows = pl.ds(pl.multiple_of((j % 2) * nq, nq), nq)
    s = lax.dot_general(k_ref[0, j // 2, own_rows, :], qz, _NT, preferred_element_type=F32)
    krow = lax.broadcasted_iota(jnp.int32, (nq, width), 0)
    qcol = lax.broadcasted_iota(jnp.int32, (nq, width), 1) & (nq - 1)
    s = jnp.where(krow <= qcol, s, -jnp.inf)
    m = jnp.max(s, axis=0, keepdims=True)
    pb = jnp.exp2(s - m).astype(BF16)
    acc = tuple(jnp.dot(vt_ref[0, j, e], head_cols(pb, e), preferred_element_type=F32)
                for e in heads)

    def accumulate(i, alpha, acc):
        return tuple(
            head_cols(alpha, e) * acc[e]
            + (jnp.dot(vt_ref[0, 2 * i, e], p_ref[:nq, e * nq:(e + 1) * nq],
                       preferred_element_type=F32)
               + jnp.dot(vt_ref[0, 2 * i + 1, e], p_ref[nq:, e * nq:(e + 1) * nq],
                         preferred_element_type=F32))
            for e in heads)

    def probabilities(m, cmax):
        m_new = jnp.maximum(m, cmax)
        p_ref[...] = jnp.exp2(s_ref[...] - m_new).astype(BF16)
        return m_new, jnp.exp2(m - m_new)

    def masked_scores(i):
        s = lax.dot_general(k_ref[0, i], qz, _NT, preferred_element_type=F32)
        s0 = s[:nq] + bias_ref[pl.ds(2 * i, 1), :]
        s1 = s[nq:] + bias_ref[pl.ds(2 * i + 1, 1), :]
        s_ref[:nq] = s0
        s_ref[nq:] = s1
        return jnp.maximum(jnp.max(s0, axis=0, keepdims=True), jnp.max(s1, axis=0, keepdims=True))

    def past_pair(i, carry):
        m, acc, cmax, alpha = carry
        acc = accumulate(jnp.maximum(i - 1, 0), alpha, acc)
        m, alpha = probabilities(m, cmax)
        return m, acc, masked_scores(i + 1), alpha

    n_pairs = (j + 1) // 2
    tail = jnp.maximum(n_pairs - 1, 0)
    p_ref[...] = jnp.zeros(p_ref.shape, BF16)
    m, acc, cmax, alpha = lax.fori_loop(
        0, tail, past_pair, (m, acc, masked_scores(0), jnp.ones((1, width), F32)))
    acc = accumulate(jnp.maximum(tail - 1, 0), alpha, acc)
    m, alpha = probabilities(m, cmax)
    acc = accumulate(tail, alpha, acc)

    o_t = jnp.concatenate([acc[e][:HEAD_DIM] / acc[e][HEAD_DIM:HEAD_DIM + 1] for e in heads],
                          axis=0)
    o_ref[0] = o_t.T.astype(BF16)


def _moba_attn(q, k_pairs, vt, km):
    b, s, _ = q.shape
    n_blk = s // MOBA_BLOCK
    n_tiles = W_MIX // LANES
    return pl.pallas_call(
        _moba_attn_kernel,
        grid=(b, n_tiles, n_blk),
        in_specs=[
            pl.BlockSpec((1, MOBA_BLOCK, LANES), lambda bi, p, j: (bi, j, p)),
            pl.BlockSpec((1, n_blk // 2, 2 * MOBA_BLOCK, LANES), lambda bi, p, j: (bi, 0, 0, p)),
            pl.BlockSpec((1, n_blk, HEADS_PER_LANE_TILE, V_AUG_ROWS, MOBA_BLOCK),
                         lambda bi, p, j: (bi, 0, p, 0, 0)),
            pl.BlockSpec((1, n_blk, LANES), lambda bi, p, j: (bi, 0, p)),
        ],
        out_specs=pl.BlockSpec((1, MOBA_BLOCK, LANES), lambda bi, p, j: (bi, j, p)),
        out_shape=jax.ShapeDtypeStruct((b, s, W_MIX), BF16),
        scratch_shapes=[
            pltpu.VMEM((n_blk, HEADS_PER_LANE_TILE * MOBA_BLOCK), F32),
            pltpu.VMEM((2 * MOBA_BLOCK, HEADS_PER_LANE_TILE * MOBA_BLOCK), F32),
            pltpu.VMEM((2 * MOBA_BLOCK, HEADS_PER_LANE_TILE * MOBA_BLOCK), BF16),
        ],
        compiler_params=pltpu.CompilerParams(
            dimension_semantics=("parallel", "parallel", "parallel"),
            vmem_limit_bytes=VMEM_LIMIT_BYTES),
        name="moba_attn",
    )(q, k_pairs, vt, km)


def _post_mixer_kernel(x_ref, mix_ref, mem_ref, woa_ref, wob_ref, g1_ref, b1_ref,
                       wg_ref, wu_ref, wd_ref, g2_ref, b2_ref, o_ref, act_ref):
    sub = (jnp.dot(mix_ref[0], woa_ref[...], preferred_element_type=F32)
           + jnp.dot(mem_ref[0], wob_ref[...], preferred_element_type=F32))
    x1 = _layer_norm(ALPHA * x_ref[0] + sub, g1_ref[...], b1_ref[...])
    x1b = x1.astype(BF16)
    d_ff = wg_ref.shape[1]
    for c in range(d_ff // FFN_CHUNK):
        cols = slice(c * FFN_CHUNK, (c + 1) * FFN_CHUNK)
        g = jnp.dot(x1b, wg_ref[:, cols], preferred_element_type=F32)
        u = jnp.dot(x1b, wu_ref[:, cols], preferred_element_type=F32)
        act_ref[:, cols] = (jax.nn.silu(g) * u).astype(BF16)
    y = jnp.dot(act_ref[...], wd_ref[...], preferred_element_type=F32)
    o_ref[0] = _layer_norm(ALPHA * x1 + y, g2_ref[...], b2_ref[...])


def _post_mixer(x, mix, mem_out, wo_a, wo_b, g1, b1, wg, wu, wd, g2, b2):
    b, s, d = x.shape
    t = TOKEN_TILE
    d_ff = wg.shape[1]
    const = lambda bi, i: (0, 0)
    single = pl.Buffered(1)

    def wspec(shape):
        return pl.BlockSpec(shape, const, pipeline_mode=single)

    return pl.pallas_call(
        _post_mixer_kernel,
        grid=(b, s // t),
        in_specs=[
            pl.BlockSpec((1, t, d), lambda bi, i: (bi, i, 0)),
            pl.BlockSpec((1, t, W_MIX), lambda bi, i: (bi, i, 0)),
            pl.BlockSpec((1, t, W_MEM), lambda bi, i: (bi, i, 0)),
            wspec(wo_a.shape), wspec(wo_b.shape), wspec(g1.shape), wspec(b1.shape),
            wspec(wg.shape), wspec(wu.shape), wspec(wd.shape), wspec(g2.shape), wspec(b2.shape),
        ],
        out_specs=pl.BlockSpec((1, t, d), lambda bi, i: (bi, i, 0)),
        out_shape=jax.ShapeDtypeStruct((b, s, d), F32),
        scratch_shapes=[pltpu.VMEM((t, d_ff), BF16)],
        compiler_params=pltpu.CompilerParams(dimension_semantics=("parallel", "parallel"),
                                             vmem_limit_bytes=VMEM_LIMIT_BYTES),
        name="post_mixer",
    )(x, mix, mem_out, wo_a, wo_b, g1, b1, wg, wu, wd, g2, b2)


def kernel(x, mem, a_w_in, a_ln_v_g, a_ln_v_b, a_w_s, a_b_s, b_w_in, w_mem_kv, w_mix_out,
           ln_mix_g, ln_mix_b, w_ffn_in, w_ffn_out, ln_ffn_g, ln_ffn_b):
    b, s, d = x.shape
    n_mem = mem.shape[1]
    d_ff = w_ffn_out.shape[1]
    assert s % TOKEN_TILE == 0 and TOKEN_TILE % MOBA_BLOCK == 0 and d_ff % FFN_CHUNK == 0
    n_blk = s // MOBA_BLOCK
    assert n_blk % 2 == 0

    kv = _mem_kv(mem.reshape(b * n_mem, d), w_mem_kv).reshape(DEPTH, b, n_mem, 2 * W_MEM)

    def post(i, xin, mix, mem_out):
        wo = w_mix_out[i].astype(BF16)
        wi = w_ffn_in[i].astype(BF16)
        row = lambda v: v.reshape(1, -1)
        return _post_mixer(xin, mix, mem_out, wo[:W_MIX], wo[W_MIX:],
                           row(ln_mix_g[i]), row(ln_mix_b[i]),
                           wi[:, :d_ff], wi[:, d_ff:], w_ffn_out[i].astype(BF16),
                           row(ln_ffn_g[i]), row(ln_ffn_b[i]))

    b_full = jnp.repeat(a_b_s[0].T, HEAD_DIM, axis=1)
    mix, mem_out = _gmlp_front(x, a_w_in[0].astype(BF16), a_ln_v_g[0].reshape(1, -1),
                               a_ln_v_b[0].reshape(1, -1), a_w_s[0].astype(BF16), b_full, kv[0])
    x = post(0, x, mix, mem_out)

    w1 = b_w_in[0].astype(BF16)
    w_qkm = jnp.concatenate([w1[:, :2 * W_MIX], w1[:, 3 * W_MIX:]], axis=1)
    w_vt = w1[:, 2 * W_MIX:3 * W_MIX].T
    q, k, vt, km, mem_out = _moba_front(x, w_qkm, w_vt, kv[1])
    mix = _moba_attn(q, k.reshape(b, n_blk // 2, 2 * MOBA_BLOCK, W_MIX), vt,
                     km.reshape(b, n_blk, W_MIX))
    return post(1, x, mix, mem_out)
```

```python
import jax
import jax.numpy as jnp
from jax import lax
from jax.experimental import pallas as pl
from jax.experimental.pallas import tpu as pltpu

HEAD_DIM = 64
N_MIX_HEADS = 12
W_MIX = N_MIX_HEADS * HEAD_DIM
N_MEM_HEADS = 4
W_MEM = N_MEM_HEADS * HEAD_DIM
GMLP_CHUNK = 128
MOBA_BLOCK = 256
MOBA_TOPK = 3
DEPTH = 2
ALPHA = (2.0 * DEPTH) ** 0.25
LN_EPS = 1e-5
ATTN_SCALE = HEAD_DIM ** -0.5
LOG2_E = 1.4426950408889634
BF16_SUBLANES = 16
V_AUG_ROWS = HEAD_DIM + BF16_SUBLANES

LANES = 128
HEADS_PER_LANE_TILE = LANES // HEAD_DIM
TOKEN_TILE = 512
FFN_CHUNK = 256
VMEM_LIMIT_BYTES = 56 * 1024 * 1024

BF16 = jnp.bfloat16
F32 = jnp.float32

_NT = (((1,), (1,)), ((), ()))


def _layer_norm(y, g, b):
    mu = jnp.mean(y, axis=-1, keepdims=True)
    d = y - mu
    var = jnp.mean(d * d, axis=-1, keepdims=True)
    return d * lax.rsqrt(var + LN_EPS) * g + b


def _head_lane_mask(e):
    lane = lax.broadcasted_iota(jnp.int32, (1, LANES), 1)
    return (lane >= e * HEAD_DIM) & (lane < (e + 1) * HEAD_DIM)


def _mem_cross_attention(qm, kv_ref):
    qb = (qm * ATTN_SCALE).astype(BF16)
    v_all = kv_ref[:, W_MEM:]
    lane_v = lax.broadcasted_iota(jnp.int32, (1, W_MEM), 1)
    out = None
    for hm in range(N_MEM_HEADS):
        tile, e = divmod(hm, HEADS_PER_LANE_TILE)
        qp = qb[:, tile * LANES:(tile + 1) * LANES]
        qz = jnp.where(_head_lane_mask(e), qp, jnp.zeros_like(qp))
        kp = kv_ref[:, tile * LANES:(tile + 1) * LANES]
        s = lax.dot_general(qz, kp, _NT, preferred_element_type=F32)
        m = jnp.max(s, axis=-1, keepdims=True)
        p = jnp.exp(s - m)
        l = jnp.sum(p, axis=-1, keepdims=True)
        pn = (p * (1.0 / l)).astype(BF16)
        vz = jnp.where((lane_v >= hm * HEAD_DIM) & (lane_v < (hm + 1) * HEAD_DIM),
                       v_all, jnp.zeros_like(v_all))
        o = jnp.dot(pn, vz, preferred_element_type=F32)
        out = o if out is None else out + o
    return out


def _mem_kv_kernel(mem_ref, w_ref, o_ref):
    o_ref[0] = jnp.dot(mem_ref[...].astype(BF16), w_ref[0].astype(BF16),
                       preferred_element_type=F32).astype(BF16)


def _mem_kv(mem2d, w_mem_kv):
    rows, d = mem2d.shape
    depth, _, n = w_mem_kv.shape
    return pl.pallas_call(
        _mem_kv_kernel,
        grid=(depth,),
        in_specs=[pl.BlockSpec((rows, d), lambda i: (0, 0)),
                  pl.BlockSpec((1, d, n), lambda i: (i, 0, 0))],
        out_specs=pl.BlockSpec((1, rows, n), lambda i: (i, 0, 0)),
        out_shape=jax.ShapeDtypeStruct((depth, rows, n), BF16),
        compiler_params=pltpu.CompilerParams(dimension_semantics=("parallel",),
                                             vmem_limit_bytes=VMEM_LIMIT_BYTES),
        name="mem_kv",
    )(mem2d, w_mem_kv)


def _gmlp_front_kernel(x_ref, w_ref, lng_ref, lnb_ref, ws_ref, bs_ref, kv_ref,
                       mix_ref, mem_ref):
    t = x_ref.shape[1]
    xb = x_ref[0].astype(BF16)
    u = jax.nn.gelu(jnp.dot(xb, w_ref[:, :W_MIX], preferred_element_type=F32))
    gv = jax.nn.gelu(jnp.dot(xb, w_ref[:, W_MIX:2 * W_MIX], preferred_element_type=F32))
    vb = _layer_norm(gv, lng_ref[...], lnb_ref[...]).astype(BF16)
    qm = jnp.dot(xb, w_ref[:, 2 * W_MIX:], preferred_element_type=F32)

    row = lax.broadcasted_iota(jnp.int32, (GMLP_CHUNK, GMLP_CHUNK), 0)
    col = lax.broadcasted_iota(jnp.int32, (GMLP_CHUNK, GMLP_CHUNK), 1)
    causal = col <= row
    m0 = _head_lane_mask(0)
    m1 = _head_lane_mask(1)
    for p in range(N_MIX_HEADS // HEADS_PER_LANE_TILE):
        w0 = ws_ref[2 * p]
        w1 = ws_ref[2 * p + 1]
        wcat = jnp.concatenate([jnp.where(causal, w0, jnp.zeros_like(w0)),
                                jnp.where(causal, w1, jnp.zeros_like(w1))], axis=1)
        bias = bs_ref[:, p * LANES:(p + 1) * LANES]
        for c in range(t // GMLP_CHUNK):
            rows = slice(c * GMLP_CHUNK, (c + 1) * GMLP_CHUNK)
            vp = vb[rows, p * LANES:(p + 1) * LANES]
            zero = jnp.zeros_like(vp)
            vbd = jnp.concatenate([jnp.where(m0, vp, zero), jnp.where(m1, vp, zero)], axis=0)
            sv = jnp.dot(wcat, vbd, preferred_element_type=F32) + bias
            mix_ref[0, rows, p * LANES:(p + 1) * LANES] = (
                u[rows, p * LANES:(p + 1) * LANES] * sv).astype(BF16)

    mem_ref[0] = _mem_cross_attention(qm, kv_ref.at[0]).astype(BF16)


def _gmlp_front(x, w_in, ln_g, ln_b, w_s, b_full, kv):
    b, s, d = x.shape
    t = TOKEN_TILE
    n_in = w_in.shape[1]
    return pl.pallas_call(
        _gmlp_front_kernel,
        grid=(b, s // t),
        in_specs=[
            pl.BlockSpec((1, t, d), lambda bi, i: (bi, i, 0)),
            pl.BlockSpec((d, n_in), lambda bi, i: (0, 0)),
            pl.BlockSpec((1, W_MIX), lambda bi, i: (0, 0)),
            pl.BlockSpec((1, W_MIX), lambda bi, i: (0, 0)),
            pl.BlockSpec((N_MIX_HEADS, GMLP_CHUNK, GMLP_CHUNK), lambda bi, i: (0, 0, 0)),
            pl.BlockSpec((GMLP_CHUNK, W_MIX), lambda bi, i: (0, 0)),
            pl.BlockSpec((1, kv.shape[1], kv.shape[2]), lambda bi, i: (bi, 0, 0)),
        ],
        out_specs=[pl.BlockSpec((1, t, W_MIX), lambda bi, i: (bi, i, 0)),
                   pl.BlockSpec((1, t, W_MEM), lambda bi, i: (bi, i, 0))],
        out_shape=[jax.ShapeDtypeStruct((b, s, W_MIX), BF16),
                   jax.ShapeDtypeStruct((b, s, W_MEM), BF16)],
        compiler_params=pltpu.CompilerParams(dimension_semantics=("parallel", "parallel"),
                                             vmem_limit_bytes=VMEM_LIMIT_BYTES),
        name="gmlp_front",
    )(x, w_in, ln_g, ln_b, w_s, b_full, kv)


def _moba_front_kernel(x_ref, wqkm_ref, wvt_ref, kv_ref,
                       q_ref, k_ref, vt_ref, km_ref, mem_ref):
    t = x_ref.shape[1]
    xb = x_ref[0].astype(BF16)
    q = jnp.dot(xb, wqkm_ref[:, :W_MIX], preferred_element_type=F32)
    q_ref[0] = (q * (ATTN_SCALE * LOG2_E)).astype(BF16)
    k = jnp.dot(xb, wqkm_ref[:, W_MIX:2 * W_MIX], preferred_element_type=F32)
    k_ref[0] = k.astype(BF16)
    for c in range(t // MOBA_BLOCK):
        km_ref[0, 0, c:c + 1, :] = jnp.mean(k[c * MOBA_BLOCK:(c + 1) * MOBA_BLOCK], axis=0,
                                            keepdims=True)
    vt = lax.dot_general(wvt_ref[...], xb, _NT, preferred_element_type=F32)
    ones = jnp.ones((V_AUG_ROWS - HEAD_DIM, MOBA_BLOCK), BF16)
    for c in range(t // MOBA_BLOCK):
        for h in range(N_MIX_HEADS):
            vt_ref[0, c, h, :HEAD_DIM, :] = vt[h * HEAD_DIM:(h + 1) * HEAD_DIM,
                                               c * MOBA_BLOCK:(c + 1) * MOBA_BLOCK].astype(BF16)
            vt_ref[0, c, h, HEAD_DIM:, :] = ones
    qm = jnp.dot(xb, wqkm_ref[:, 2 * W_MIX:], preferred_element_type=F32)
    mem_ref[0] = _mem_cross_attention(qm, kv_ref.at[0]).astype(BF16)


def _moba_front(x, w_qkm, w_vt, kv):
    b, s, d = x.shape
    t = TOKEN_TILE
    bpt = t // MOBA_BLOCK
    n_blk = s // MOBA_BLOCK
    return pl.pallas_call(
        _moba_front_kernel,
        grid=(b, s // t),
        in_specs=[
            pl.BlockSpec((1, t, d), lambda bi, i: (bi, i, 0)),
            pl.BlockSpec(w_qkm.shape, lambda bi, i: (0, 0)),
            pl.BlockSpec(w_vt.shape, lambda bi, i: (0, 0)),
            pl.BlockSpec((1, kv.shape[1], kv.shape[2]), lambda bi, i: (bi, 0, 0)),
        ],
        out_specs=[
            pl.BlockSpec((1, t, W_MIX), lambda bi, i: (bi, i, 0)),
            pl.BlockSpec((1, t, W_MIX), lambda bi, i: (bi, i, 0)),
            pl.BlockSpec((1, bpt, N_MIX_HEADS, V_AUG_ROWS, MOBA_BLOCK),
                         lambda bi, i: (bi, i, 0, 0, 0)),
            pl.BlockSpec((1, 1, bpt, W_MIX), lambda bi, i: (bi, i, 0, 0)),
            pl.BlockSpec((1, t, W_MEM), lambda bi, i: (bi, i, 0)),
        ],
        out_shape=[
            jax.ShapeDtypeStruct((b, s, W_MIX), BF16),
            jax.ShapeDtypeStruct((b, s, W_MIX), BF16),
            jax.ShapeDtypeStruct((b, n_blk, N_MIX_HEADS, V_AUG_ROWS, MOBA_BLOCK), BF16),
            jax.ShapeDtypeStruct((b, s // t, bpt, W_MIX), F32),
            jax.ShapeDtypeStruct((b, s, W_MEM), BF16),
        ],
        compiler_params=pltpu.CompilerParams(dimension_semantics=("parallel", "parallel"),
                                             vmem_limit_bytes=VMEM_LIMIT_BYTES),
        name="moba_front",
    )(x, w_qkm, w_vt, kv)


def _select_blocks(bs_t, n_past):
    n_blk = bs_t.shape[0]
    nidx = lax.broadcasted_iota(jnp.int32, bs_t.shape, 0)
    rem = nidx < n_past
    sel = jnp.zeros(bs_t.shape, jnp.bool_)
    for _ in range(MOBA_TOPK):
        ms = jnp.where(rem, bs_t, -jnp.inf)
        mx = jnp.max(ms, axis=0, keepdims=True)
        is_max = rem & (ms == mx)
        first = jnp.min(jnp.where(is_max, nidx, n_blk), axis=0, keepdims=True)
        pick = is_max & (nidx == first)
        sel = sel | pick
        rem = rem & jnp.logical_not(pick)
    return sel


def _moba_work_items(n_blk):
    items = []
    for j in range(n_blk):
        own = j // 2
        items.append(j | own << 8 | (j % 2 == 0) << 16 | (j % 2 == 1) << 17 | 1 << 18)
        items.extend(j | i << 8 for i in range(own))
    assert len(items) % 2 == 0
    items.append(items[-1])
    return items


def _moba_attn_kernel(tab_ref, q_ref, k_ref, vt_ref, km_ref, o_ref,
                      qz_ref, bias_ref, cz_ref, sa_ref, sb_ref, p_ref, sums_ref):
    nq = MOBA_BLOCK
    heads = range(HEADS_PER_LANE_TILE)
    width = HEADS_PER_LANE_TILE * nq
    n_blk = bias_ref.shape[0]
    n_items = tab_ref.shape[0] - 1

    def head_cols(a, e):
        return a[:, e * nq:(e + 1) * nq]

    def query_rows(tile):
        return pl.ds(pl.multiple_of(tile * nq, nq), nq)

    krow = lax.broadcasted_iota(jnp.int32, (nq, width), 0)
    qcol = lax.broadcasted_iota(jnp.int32, (nq, width), 1) & (nq - 1)
    cz_ref[0] = jnp.zeros((nq, width), F32)
    cz_ref[1] = jnp.where(krow <= qcol, 0.0, -jnp.inf).astype(F32)

    kmb = km_ref[0].astype(BF16)
    nidx = lax.broadcasted_iota(jnp.int32, (n_blk, width), 0)

    def select(j, carry):
        qp = q_ref[0, query_rows(j), :]
        zero = jnp.zeros_like(qp)
        qz = jnp.concatenate([jnp.where(_head_lane_mask(e), qp, zero) for e in heads], axis=0)
        qz_ref[j] = qz
        bs_t = lax.dot_general(kmb, qz, _NT, preferred_element_type=F32)
        keep = _select_blocks(bs_t, j) | (nidx == j)
        bias_ref[j] = jnp.where(keep, 0.0, -jnp.inf).astype(F32)
        return carry

    lax.fori_loop(0, n_blk, select, 0)

    def decode(w):
        code = tab_ref[w]
        return code & 0xFF, (code >> 8) & 0xFF, (code >> 16) & 1, (code >> 17) & 1, (code >> 18) & 1

    def block_bias(w):
        tile, pair = decode(w)[:2]
        return bias_ref[tile, pl.ds(2 * pair, 1), :], bias_ref[tile, pl.ds(2 * pair + 1, 1), :]

    def masked_scores(w, s_buf):
        tile, pair, causal_lo, causal_hi, _ = decode(w)
        s = lax.dot_general(k_ref[0, pair], qz_ref[tile], _NT, preferred_element_type=F32)
        s0 = s[:nq] + cz_ref[causal_lo]
        s1 = s[nq:] + cz_ref[causal_hi]
        s_buf[:nq] = s0
        s_buf[nq:] = s1
        b0, b1 = block_bias(w)
        return jnp.maximum(jnp.max(s0, axis=0, keepdims=True) + b0,
                           jnp.max(s1, axis=0, keepdims=True) + b1)

    def probabilities(w, m, cmax, s_buf):
        first = decode(w)[4] == 1
        m = jnp.where(first, jnp.full_like(m, -1e30), m)
        m_new = jnp.maximum(m, cmax)
        b0, b1 = block_bias(w)
        p_ref[:nq] = jnp.exp2(s_buf[:nq] - (m_new - b0)).astype(BF16)
        p_ref[nq:] = jnp.exp2(s_buf[nq:] - (m_new - b1)).astype(BF16)
        return m_new, jnp.where(first, jnp.zeros_like(m), jnp.exp2(m - m_new))

    def accumulate(w, alpha, acc):
        tile, pair = decode(w)[:2]
        acc = tuple(
            head_cols(alpha, e) * acc[e]
            + (jnp.dot(vt_ref[0, 2 * pair, e], p_ref[:nq, e * nq:(e + 1) * nq],
                       preferred_element_type=F32)
               + jnp.dot(vt_ref[0, 2 * pair + 1, e], p_ref[nq:, e * nq:(e + 1) * nq],
                         preferred_element_type=F32))
            for e in heads)
        for e in heads:
            sums_ref[tile, e] = acc[e]
        return acc

    def two_items(t, carry):
        m, acc, cmax_a, alpha = carry
        w = 2 * t
        cmax_b = masked_scores(w + 1, sb_ref)
        acc = accumulate(jnp.maximum(w - 1, 0), alpha, acc)
        m, alpha = probabilities(w, m, cmax_a, sa_ref)
        cmax_a = masked_scores(w + 2, sa_ref)
        acc = accumulate(w, alpha, acc)
        m, alpha = probabilities(w + 1, m, cmax_b, sb_ref)
        return m, acc, cmax_a, alpha

    p_ref[...] = jnp.zeros(p_ref.shape, BF16)
    carry = (jnp.zeros((1, width), F32), tuple(jnp.ones((V_AUG_ROWS, nq), F32) for _ in heads),
             masked_scores(0, sa_ref), jnp.ones((1, width), F32))
    m, acc, _, alpha = lax.fori_loop(0, n_items // 2, two_items, carry)
    accumulate(n_items - 1, alpha, acc)

    def normalise(j, carry):
        o_t = jnp.concatenate(
            [sums_ref[j, e, :HEAD_DIM] / sums_ref[j, e, HEAD_DIM:HEAD_DIM + 1] for e in heads],
            axis=0)
        o_ref[0, query_rows(j), :] = o_t.T.astype(BF16)
        return carry

    lax.fori_loop(0, n_blk, normalise, 0)


def _moba_attn(q, k_pairs, vt, km):
    b, s, _ = q.shape
    n_blk = s // MOBA_BLOCK
    n_tiles = W_MIX // LANES
    width = HEADS_PER_LANE_TILE * MOBA_BLOCK
    table = jnp.asarray(_moba_work_items(n_blk), jnp.int32)
    grid_spec = pltpu.PrefetchScalarGridSpec(
        num_scalar_prefetch=1,
        grid=(b, n_tiles),
        in_specs=[
            pl.BlockSpec((1, s, LANES), lambda bi, p, tab: (bi, 0, p)),
            pl.BlockSpec((1, n_blk // 2, 2 * MOBA_BLOCK, LANES), lambda bi, p, tab: (bi, 0, 0, p)),
            pl.BlockSpec((1, n_blk, HEADS_PER_LANE_TILE, V_AUG_ROWS, MOBA_BLOCK),
                         lambda bi, p, tab: (bi, 0, p, 0, 0)),
            pl.BlockSpec((1, n_blk, LANES), lambda bi, p, tab: (bi, 0, p)),
        ],
        out_specs=pl.BlockSpec((1, s, LANES), lambda bi, p, tab: (bi, 0, p)),
        scratch_shapes=[
            pltpu.VMEM((n_blk, width, LANES), BF16),
            pltpu.VMEM((n_blk, n_blk, width), F32),
            pltpu.VMEM((2, MOBA_BLOCK, width), F32),
            pltpu.VMEM((2 * MOBA_BLOCK, width), F32),
            pltpu.VMEM((2 * MOBA_BLOCK, width), F32),
            pltpu.VMEM((2 * MOBA_BLOCK, width), BF16),
            pltpu.VMEM((n_blk, HEADS_PER_LANE_TILE, V_AUG_ROWS, MOBA_BLOCK), F32),
        ],
    )
    return pl.pallas_call(
        _moba_attn_kernel,
        grid_spec=grid_spec,
        out_shape=jax.ShapeDtypeStruct((b, s, W_MIX), BF16),
        compiler_params=pltpu.CompilerParams(
            dimension_semantics=("parallel", "parallel"),
            vmem_limit_bytes=VMEM_LIMIT_BYTES),
        name="moba_attn",
    )(table, q, k_pairs, vt, km)


def _post_mixer_kernel(x_ref, mix_ref, mem_ref, woa_ref, wob_ref, g1_ref, b1_ref,
                       wg_ref, wu_ref, wd_ref, g2_ref, b2_ref, o_ref, act_ref):
    sub = (jnp.dot(mix_ref[0], woa_ref[...], preferred_element_type=F32)
           + jnp.dot(mem_ref[0], wob_ref[...], preferred_element_type=F32))
    x1 = _layer_norm(ALPHA * x_ref[0] + sub, g1_ref[...], b1_ref[...])
    x1b = x1.astype(BF16)
    d_ff = wg_ref.shape[1]
    for c in range(d_ff // FFN_CHUNK):
        cols = slice(c * FFN_CHUNK, (c + 1) * FFN_CHUNK)
        g = jnp.dot(x1b, wg_ref[:, cols], preferred_element_type=F32)
        u = jnp.dot(x1b, wu_ref[:, cols], preferred_element_type=F32)
        act_ref[:, cols] = (jax.nn.silu(g) * u).astype(BF16)
    y = jnp.dot(act_ref[...], wd_ref[...], preferred_element_type=F32)
    o_ref[0] = _layer_norm(ALPHA * x1 + y, g2_ref[...], b2_ref[...])


def _post_mixer(x, mix, mem_out, wo_a, wo_b, g1, b1, wg, wu, wd, g2, b2):
    b, s, d = x.shape
    t = TOKEN_TILE
    d_ff = wg.shape[1]
    const = lambda bi, i: (0, 0)
    single = pl.Buffered(1)

    def wspec(shape):
        return pl.BlockSpec(shape, const, pipeline_mode=single)

    return pl.pallas_call(
        _post_mixer_kernel,
        grid=(b, s // t),
        in_specs=[
            pl.BlockSpec((1, t, d), lambda bi, i: (bi, i, 0)),
            pl.BlockSpec((1, t, W_MIX), lambda bi, i: (bi, i, 0)),
            pl.BlockSpec((1, t, W_MEM), lambda bi, i: (bi, i, 0)),
            wspec(wo_a.shape), wspec(wo_b.shape), wspec(g1.shape), wspec(b1.shape),
            wspec(wg.shape), wspec(wu.shape), wspec(wd.shape), wspec(g2.shape), wspec(b2.shape),
        ],
        out_specs=pl.BlockSpec((1, t, d), lambda bi, i: (bi, i, 0)),
        out_shape=jax.ShapeDtypeStruct((b, s, d), F32),
        scratch_shapes=[pltpu.VMEM((t, d_ff), BF16)],
        compiler_params=pltpu.CompilerParams(dimension_semantics=("parallel", "parallel"),
                                             vmem_limit_bytes=VMEM_LIMIT_BYTES),
        name="post_mixer",
    )(x, mix, mem_out, wo_a, wo_b, g1, b1, wg, wu, wd, g2, b2)


def kernel(x, mem, a_w_in, a_ln_v_g, a_ln_v_b, a_w_s, a_b_s, b_w_in, w_mem_kv, w_mix_out,
           ln_mix_g, ln_mix_b, w_ffn_in, w_ffn_out, ln_ffn_g, ln_ffn_b):
    b, s, d = x.shape
    n_mem = mem.shape[1]
    d_ff = w_ffn_out.shape[1]
    assert s % TOKEN_TILE == 0 and TOKEN_TILE % MOBA_BLOCK == 0 and d_ff % FFN_CHUNK == 0
    n_blk = s // MOBA_BLOCK
    assert n_blk % 2 == 0

    kv = _mem_kv(mem.reshape(b * n_mem, d), w_mem_kv).reshape(DEPTH, b, n_mem, 2 * W_MEM)

    def post(i, xin, mix, mem_out):
        wo = w_mix_out[i].astype(BF16)
        wi = w_ffn_in[i].astype(BF16)
        row = lambda v: v.reshape(1, -1)
        return _post_mixer(xin, mix, mem_out, wo[:W_MIX], wo[W_MIX:],
                           row(ln_mix_g[i]), row(ln_mix_b[i]),
                           wi[:, :d_ff], wi[:, d_ff:], w_ffn_out[i].astype(BF16),
                           row(ln_ffn_g[i]), row(ln_ffn_b[i]))

    b_full = jnp.repeat(a_b_s[0].T, HEAD_DIM, axis=1)
    mix, mem_out = _gmlp_front(x, a_w_in[0].astype(BF16), a_ln_v_g[0].reshape(1, -1),
                               a_ln_v_b[0].reshape(1, -1), a_w_s[0].astype(BF16), b_full, kv[0])
    x = post(0, x, mix, mem_out)

    w1 = b_w_in[0].astype(BF16)
    w_qkm = jnp.concatenate([w1[:, :2 * W_MIX], w1[:, 3 * W_MIX:]], axis=1)
    w_vt = w1[:, 2 * W_MIX:3 * W_MIX].T
    q, k, vt, km, mem_out = _moba_front(x, w_qkm, w_vt, kv[1])
    mix = _moba_attn(q, k.reshape(b, n_blk // 2, 2 * MOBA_BLOCK, W_MIX), vt,
                     km.reshape(b, n_blk, W_MIX))
    return post(1, x, mix, mem_out)
```

```python
import jax
import jax.numpy as jnp
from jax import lax
from jax.experimental import pallas as pl
from jax.experimental.pallas import tpu as pltpu

HEAD_DIM = 64
N_MIX_HEADS = 12
W_MIX = N_MIX_HEADS * HEAD_DIM
N_MEM_HEADS = 4
W_MEM = N_MEM_HEADS * HEAD_DIM
GMLP_CHUNK = 128
MOBA_BLOCK = 256
MOBA_TOPK = 3
DEPTH = 2
ALPHA = (2.0 * DEPTH) ** 0.25
LN_EPS = 1e-5
ATTN_SCALE = HEAD_DIM ** -0.5
LOG2_E = 1.4426950408889634
RUNNING_MAX_FLOOR = -1e30
BF16_SUBLANES = 16
V_AUG_ROWS = HEAD_DIM + BF16_SUBLANES

LANES = 128
HEADS_PER_LANE_TILE = LANES // HEAD_DIM
TOKEN_TILE = 512
FFN_CHUNK = 256
VMEM_LIMIT_BYTES = 56 * 1024 * 1024

BF16 = jnp.bfloat16
F32 = jnp.float32

_NT = (((1,), (1,)), ((), ()))


def _layer_norm(y, g, b):
    mu = jnp.mean(y, axis=-1, keepdims=True)
    d = y - mu
    var = jnp.mean(d * d, axis=-1, keepdims=True)
    return d * lax.rsqrt(var + LN_EPS) * g + b


def _head_lane_mask(e):
    lane = lax.broadcasted_iota(jnp.int32, (1, LANES), 1)
    return (lane >= e * HEAD_DIM) & (lane < (e + 1) * HEAD_DIM)


def _mem_cross_attention(qm, kv_ref):
    qb = (qm * ATTN_SCALE).astype(BF16)
    v_all = kv_ref[:, W_MEM:]
    lane_v = lax.broadcasted_iota(jnp.int32, (1, W_MEM), 1)
    out = None
    for hm in range(N_MEM_HEADS):
        tile, e = divmod(hm, HEADS_PER_LANE_TILE)
        qp = qb[:, tile * LANES:(tile + 1) * LANES]
        qz = jnp.where(_head_lane_mask(e), qp, jnp.zeros_like(qp))
        kp = kv_ref[:, tile * LANES:(tile + 1) * LANES]
        s = lax.dot_general(qz, kp, _NT, preferred_element_type=F32)
        m = jnp.max(s, axis=-1, keepdims=True)
        p = jnp.exp(s - m)
        l = jnp.sum(p, axis=-1, keepdims=True)
        pn = (p * (1.0 / l)).astype(BF16)
        vz = jnp.where((lane_v >= hm * HEAD_DIM) & (lane_v < (hm + 1) * HEAD_DIM),
                       v_all, jnp.zeros_like(v_all))
        o = jnp.dot(pn, vz, preferred_element_type=F32)
        out = o if out is None else out + o
    return out


def _mem_kv_kernel(mem_ref, w_ref, o_ref):
    o_ref[0] = jnp.dot(mem_ref[...].astype(BF16), w_ref[0].astype(BF16),
                       preferred_element_type=F32).astype(BF16)


def _mem_kv(mem2d, w_mem_kv):
    rows, d = mem2d.shape
    depth, _, n = w_mem_kv.shape
    return pl.pallas_call(
        _mem_kv_kernel,
        grid=(depth,),
        in_specs=[pl.BlockSpec((rows, d), lambda i: (0, 0)),
                  pl.BlockSpec((1, d, n), lambda i: (i, 0, 0))],
        out_specs=pl.BlockSpec((1, rows, n), lambda i: (i, 0, 0)),
        out_shape=jax.ShapeDtypeStruct((depth, rows, n), BF16),
        compiler_params=pltpu.CompilerParams(dimension_semantics=("parallel",),
                                             vmem_limit_bytes=VMEM_LIMIT_BYTES),
        name="mem_kv",
    )(mem2d, w_mem_kv)


def _gmlp_front_kernel(x_ref, w_ref, lng_ref, lnb_ref, ws_ref, bs_ref, kv_ref,
                       mix_ref, mem_ref):
    t = x_ref.shape[1]
    xb = x_ref[0].astype(BF16)
    u = jax.nn.gelu(jnp.dot(xb, w_ref[:, :W_MIX], preferred_element_type=F32))
    gv = jax.nn.gelu(jnp.dot(xb, w_ref[:, W_MIX:2 * W_MIX], preferred_element_type=F32))
    vb = _layer_norm(gv, lng_ref[...], lnb_ref[...]).astype(BF16)
    qm = jnp.dot(xb, w_ref[:, 2 * W_MIX:], preferred_element_type=F32)

    row = lax.broadcasted_iota(jnp.int32, (GMLP_CHUNK, GMLP_CHUNK), 0)
    col = lax.broadcasted_iota(jnp.int32, (GMLP_CHUNK, GMLP_CHUNK), 1)
    causal = col <= row
    m0 = _head_lane_mask(0)
    m1 = _head_lane_mask(1)
    for p in range(N_MIX_HEADS // HEADS_PER_LANE_TILE):
        w0 = ws_ref[2 * p]
        w1 = ws_ref[2 * p + 1]
        wcat = jnp.concatenate([jnp.where(causal, w0, jnp.zeros_like(w0)),
                                jnp.where(causal, w1, jnp.zeros_like(w1))], axis=1)
        bias = bs_ref[:, p * LANES:(p + 1) * LANES]
        for c in range(t // GMLP_CHUNK):
            rows = slice(c * GMLP_CHUNK, (c + 1) * GMLP_CHUNK)
            vp = vb[rows, p * LANES:(p + 1) * LANES]
            zero = jnp.zeros_like(vp)
            vbd = jnp.concatenate([jnp.where(m0, vp, zero), jnp.where(m1, vp, zero)], axis=0)
            sv = jnp.dot(wcat, vbd, preferred_element_type=F32) + bias
            mix_ref[0, rows, p * LANES:(p + 1) * LANES] = (
                u[rows, p * LANES:(p + 1) * LANES] * sv).astype(BF16)

    mem_ref[0] = _mem_cross_attention(qm, kv_ref.at[0]).astype(BF16)


def _gmlp_front(x, w_in, ln_g, ln_b, w_s, b_full, kv):
    b, s, d = x.shape
    t = TOKEN_TILE
    n_in = w_in.shape[1]
    return pl.pallas_call(
        _gmlp_front_kernel,
        grid=(b, s // t),
        in_specs=[
            pl.BlockSpec((1, t, d), lambda bi, i: (bi, i, 0)),
            pl.BlockSpec((d, n_in), lambda bi, i: (0, 0)),
            pl.BlockSpec((1, W_MIX), lambda bi, i: (0, 0)),
            pl.BlockSpec((1, W_MIX), lambda bi, i: (0, 0)),
            pl.BlockSpec((N_MIX_HEADS, GMLP_CHUNK, GMLP_CHUNK), lambda bi, i: (0, 0, 0)),
            pl.BlockSpec((GMLP_CHUNK, W_MIX), lambda bi, i: (0, 0)),
            pl.BlockSpec((1, kv.shape[1], kv.shape[2]), lambda bi, i: (bi, 0, 0)),
        ],
        out_specs=[pl.BlockSpec((1, t, W_MIX), lambda bi, i: (bi, i, 0)),
                   pl.BlockSpec((1, t, W_MEM), lambda bi, i: (bi, i, 0))],
        out_shape=[jax.ShapeDtypeStruct((b, s, W_MIX), BF16),
                   jax.ShapeDtypeStruct((b, s, W_MEM), BF16)],
        compiler_params=pltpu.CompilerParams(dimension_semantics=("parallel", "parallel"),
                                             vmem_limit_bytes=VMEM_LIMIT_BYTES),
        name="gmlp_front",
    )(x, w_in, ln_g, ln_b, w_s, b_full, kv)


def _moba_front_kernel(x_ref, wqkm_ref, wvt_ref, kv_ref,
                       q_ref, k_ref, vt_ref, km_ref, mem_ref):
    t = x_ref.shape[1]
    xb = x_ref[0].astype(BF16)
    q = jnp.dot(xb, wqkm_ref[:, :W_MIX], preferred_element_type=F32)
    q_ref[0] = (q * (ATTN_SCALE * LOG2_E)).astype(BF16)
    k = jnp.dot(xb, wqkm_ref[:, W_MIX:2 * W_MIX], preferred_element_type=F32)
    k_ref[0] = k.astype(BF16)
    for c in range(t // MOBA_BLOCK):
        km_ref[0, 0, c:c + 1, :] = jnp.mean(k[c * MOBA_BLOCK:(c + 1) * MOBA_BLOCK], axis=0,
                                            keepdims=True)
    vt = lax.dot_general(wvt_ref[...], xb, _NT, preferred_element_type=F32)
    ones = jnp.ones((V_AUG_ROWS - HEAD_DIM, MOBA_BLOCK), BF16)
    for c in range(t // MOBA_BLOCK):
        for h in range(N_MIX_HEADS):
            vt_ref[0, c, h, :HEAD_DIM, :] = vt[h * HEAD_DIM:(h + 1) * HEAD_DIM,
                                               c * MOBA_BLOCK:(c + 1) * MOBA_BLOCK].astype(BF16)
            vt_ref[0, c, h, HEAD_DIM:, :] = ones
    qm = jnp.dot(xb, wqkm_ref[:, 2 * W_MIX:], preferred_element_type=F32)
    mem_ref[0] = _mem_cross_attention(qm, kv_ref.at[0]).astype(BF16)


def _moba_front(x, w_qkm, w_vt, kv):
    b, s, d = x.shape
    t = TOKEN_TILE
    bpt = t // MOBA_BLOCK
    n_blk = s // MOBA_BLOCK
    return pl.pallas_call(
        _moba_front_kernel,
        grid=(b, s // t),
        in_specs=[
            pl.BlockSpec((1, t, d), lambda bi, i: (bi, i, 0)),
            pl.BlockSpec(w_qkm.shape, lambda bi, i: (0, 0)),
            pl.BlockSpec(w_vt.shape, lambda bi, i: (0, 0)),
            pl.BlockSpec((1, kv.shape[1], kv.shape[2]), lambda bi, i: (bi, 0, 0)),
        ],
        out_specs=[
            pl.BlockSpec((1, t, W_MIX), lambda bi, i: (bi, i, 0)),
            pl.BlockSpec((1, t, W_MIX), lambda bi, i: (bi, i, 0)),
            pl.BlockSpec((1, bpt, N_MIX_HEADS, V_AUG_ROWS, MOBA_BLOCK),
                         lambda bi, i: (bi, i, 0, 0, 0)),
            pl.BlockSpec((1, 1, bpt, W_MIX), lambda bi, i: (bi, i, 0, 0)),
            pl.BlockSpec((1, t, W_MEM), lambda bi, i: (bi, i, 0)),
        ],
        out_shape=[
            jax.ShapeDtypeStruct((b, s, W_MIX), BF16),
            jax.ShapeDtypeStruct((b, s, W_MIX), BF16),
            jax.ShapeDtypeStruct((b, n_blk, N_MIX_HEADS, V_AUG_ROWS, MOBA_BLOCK), BF16),
            jax.ShapeDtypeStruct((b, s // t, bpt, W_MIX), F32),
            jax.ShapeDtypeStruct((b, s, W_MEM), BF16),
        ],
        compiler_params=pltpu.CompilerParams(dimension_semantics=("parallel", "parallel"),
                                             vmem_limit_bytes=VMEM_LIMIT_BYTES),
        name="moba_front",
    )(x, w_qkm, w_vt, kv)


def _select_blocks(bs_t, n_past):
    n_blk = bs_t.shape[0]
    nidx = lax.broadcasted_iota(jnp.int32, bs_t.shape, 0)
    rem = nidx < n_past
    sel = jnp.zeros(bs_t.shape, jnp.bool_)
    for _ in range(MOBA_TOPK):
        ms = jnp.where(rem, bs_t, -jnp.inf)
        mx = jnp.max(ms, axis=0, keepdims=True)
        is_max = rem & (ms == mx)
        first = jnp.min(jnp.where(is_max, nidx, n_blk), axis=0, keepdims=True)
        pick = is_max & (nidx == first)
        sel = sel | pick
        rem = rem & jnp.logical_not(pick)
    return sel


def _moba_work_items(n_blk):
    items = []
    for j in range(n_blk):
        own = j // 2
        items.append(j | own << 8 | (j % 2 == 0) << 16 | (j % 2 == 1) << 17 | 1 << 18)
        items.extend(j | i << 8 for i in range(own))
    items.append(items[-1])
    return items


def _moba_attn_kernel(tab_ref, q_ref, k_ref, vt_ref, km_ref, o_ref,
                      qz_ref, bias_ref, cz_ref, s_ref, p_ref, sums_ref):
    nq = MOBA_BLOCK
    heads = range(HEADS_PER_LANE_TILE)
    width = HEADS_PER_LANE_TILE * nq
    n_blk = bias_ref.shape[0]
    n_items = tab_ref.shape[0] - 1

    def head_cols(a, e):
        return a[:, e * nq:(e + 1) * nq]

    def query_rows(tile):
        return pl.ds(pl.multiple_of(tile * nq, nq), nq)

    krow = lax.broadcasted_iota(jnp.int32, (nq, width), 0)
    qcol = lax.broadcasted_iota(jnp.int32, (nq, width), 1) & (nq - 1)
    cz_ref[0] = jnp.zeros((nq, width), F32)
    cz_ref[1] = jnp.where(krow <= qcol, 0.0, -jnp.inf).astype(F32)

    kmb = km_ref[0].astype(BF16)
    nidx = lax.broadcasted_iota(jnp.int32, (n_blk, width), 0)

    def select(j, carry):
        qp = q_ref[0, query_rows(j), :]
        zero = jnp.zeros_like(qp)
        qz = jnp.concatenate([jnp.where(_head_lane_mask(e), qp, zero) for e in heads], axis=0)
        qz_ref[j] = qz
        bs_t = lax.dot_general(kmb, qz, _NT, preferred_element_type=F32)
        keep = _select_blocks(bs_t, j) | (nidx == j)
        bias_ref[j] = jnp.where(keep, 0.0, -jnp.inf).astype(F32)
        return carry

    lax.fori_loop(0, n_blk, select, 0)

    def decode(w):
        code = tab_ref[w]
        return code & 0xFF, (code >> 8) & 0xFF, (code >> 16) & 1, (code >> 17) & 1, (code >> 18) & 1

    def block_bias(w):
        tile, pair = decode(w)[:2]
        return bias_ref[tile, pl.ds(2 * pair, 1), :], bias_ref[tile, pl.ds(2 * pair + 1, 1), :]

    def masked_scores(w, s_buf):
        tile, pair, causal_lo, causal_hi, _ = decode(w)
        s = lax.dot_general(k_ref[0, pair], qz_ref[tile], _NT, preferred_element_type=F32)
        s0 = s[:nq] + cz_ref[causal_lo]
        s1 = s[nq:] + cz_ref[causal_hi]
        s_buf[:nq] = s0
        s_buf[nq:] = s1
        b0, b1 = block_bias(w)
        return jnp.maximum(jnp.max(s0, axis=0, keepdims=True) + b0,
                           jnp.max(s1, axis=0, keepdims=True) + b1)

    def probabilities(w, m, cmax, s_buf):
        first = decode(w)[4] == 1
        m = jnp.where(first, jnp.full_like(m, RUNNING_MAX_FLOOR), m)
        m_new = jnp.maximum(m, cmax)
        b0, b1 = block_bias(w)
        p_ref[:nq] = jnp.exp2(s_buf[:nq] - (m_new - b0)).astype(BF16)
        p_ref[nq:] = jnp.exp2(s_buf[nq:] - (m_new - b1)).astype(BF16)
        return m_new, jnp.where(first, jnp.zeros_like(m), jnp.exp2(m - m_new))

    def accumulate(w, alpha, acc):
        tile, pair = decode(w)[:2]
        acc = tuple(
            head_cols(alpha, e) * acc[e]
            + (jnp.dot(vt_ref[0, 2 * pair, e], p_ref[:nq, e * nq:(e + 1) * nq],
                       preferred_element_type=F32)
               + jnp.dot(vt_ref[0, 2 * pair + 1, e], p_ref[nq:, e * nq:(e + 1) * nq],
                         preferred_element_type=F32))
            for e in heads)
        for e in heads:
            sums_ref[tile, e] = acc[e]
        return acc

    def one_item(w, carry):
        m, acc, cmax, alpha = carry
        acc = accumulate(jnp.maximum(w - 1, 0), alpha, acc)
        m, alpha = probabilities(w, m, cmax, s_ref)
        return m, acc, masked_scores(w + 1, s_ref), alpha

    p_ref[...] = jnp.zeros(p_ref.shape, BF16)
    carry = (jnp.zeros((1, width), F32), tuple(jnp.ones((V_AUG_ROWS, nq), F32) for _ in heads),
             masked_scores(0, s_ref), jnp.ones((1, width), F32))
    m, acc, _, alpha = lax.fori_loop(0, n_items, one_item, carry, unroll=2)
    accumulate(n_items - 1, alpha, acc)

    def normalise(j, carry):
        o_t = jnp.concatenate(
            [sums_ref[j, e, :HEAD_DIM] / sums_ref[j, e, HEAD_DIM:HEAD_DIM + 1] for e in heads],
            axis=0)
        o_ref[0, query_rows(j), :] = o_t.T.astype(BF16)
        return carry

    lax.fori_loop(0, n_blk, normalise, 0)


def _moba_attn(q, k_pairs, vt, km):
    b, s, _ = q.shape
    n_blk = s // MOBA_BLOCK
    n_tiles = W_MIX // LANES
    width = HEADS_PER_LANE_TILE * MOBA_BLOCK
    table = jnp.asarray(_moba_work_items(n_blk), jnp.int32)
    grid_spec = pltpu.PrefetchScalarGridSpec(
        num_scalar_prefetch=1,
        grid=(b, n_tiles),
        in_specs=[
            pl.BlockSpec((1, s, LANES), lambda bi, p, tab: (bi, 0, p)),
            pl.BlockSpec((1, n_blk // 2, 2 * MOBA_BLOCK, LANES), lambda bi, p, tab: (bi, 0, 0, p)),
            pl.BlockSpec((1, n_blk, HEADS_PER_LANE_TILE, V_AUG_ROWS, MOBA_BLOCK),
                         lambda bi, p, tab: (bi, 0, p, 0, 0)),
            pl.BlockSpec((1, n_blk, LANES), lambda bi, p, tab: (bi, 0, p)),
        ],
        out_specs=pl.BlockSpec((1, s, LANES), lambda bi, p, tab: (bi, 0, p)),
        scratch_shapes=[
            pltpu.VMEM((n_blk, width, LANES), BF16),
            pltpu.VMEM((n_blk, n_blk, width), F32),
            pltpu.VMEM((2, MOBA_BLOCK, width), F32),
            pltpu.VMEM((2 * MOBA_BLOCK, width), F32),
            pltpu.VMEM((2 * MOBA_BLOCK, width), BF16),
            pltpu.VMEM((n_blk, HEADS_PER_LANE_TILE, V_AUG_ROWS, MOBA_BLOCK), F32),
        ],
    )
    return pl.pallas_call(
        _moba_attn_kernel,
        grid_spec=grid_spec,
        out_shape=jax.ShapeDtypeStruct((b, s, W_MIX), BF16),
        compiler_params=pltpu.CompilerParams(
            dimension_semantics=("parallel", "parallel"),
            vmem_limit_bytes=VMEM_LIMIT_BYTES),
        name="moba_attn",
    )(table, q, k_pairs, vt, km)


def _post_mixer_kernel(x_ref, mix_ref, mem_ref, woa_ref, wob_ref, g1_ref, b1_ref,
                       wg_ref, wu_ref, wd_ref, g2_ref, b2_ref, o_ref, act_ref):
    sub = (jnp.dot(mix_ref[0], woa_ref[...], preferred_element_type=F32)
           + jnp.dot(mem_ref[0], wob_ref[...], preferred_element_type=F32))
    x1 = _layer_norm(ALPHA * x_ref[0] + sub, g1_ref[...], b1_ref[...])
    x1b = x1.astype(BF16)
    d_ff = wg_ref.shape[1]
    for c in range(d_ff // FFN_CHUNK):
        cols = slice(c * FFN_CHUNK, (c + 1) * FFN_CHUNK)
        g = jnp.dot(x1b, wg_ref[:, cols], preferred_element_type=F32)
        u = jnp.dot(x1b, wu_ref[:, cols], preferred_element_type=F32)
        act_ref[:, cols] = (jax.nn.silu(g) * u).astype(BF16)
    y = jnp.dot(act_ref[...], wd_ref[...], preferred_element_type=F32)
    o_ref[0] = _layer_norm(ALPHA * x1 + y, g2_ref[...], b2_ref[...])


def _post_mixer(x, mix, mem_out, wo_a, wo_b, g1, b1, wg, wu, wd, g2, b2):
    b, s, d = x.shape
    t = TOKEN_TILE
    d_ff = wg.shape[1]
    const = lambda bi, i: (0, 0)
    single = pl.Buffered(1)

    def wspec(shape):
        return pl.BlockSpec(shape, const, pipeline_mode=single)

    return pl.pallas_call(
        _post_mixer_kernel,
        grid=(b, s // t),
        in_specs=[
            pl.BlockSpec((1, t, d), lambda bi, i: (bi, i, 0)),
            pl.BlockSpec((1, t, W_MIX), lambda bi, i: (bi, i, 0)),
            pl.BlockSpec((1, t, W_MEM), lambda bi, i: (bi, i, 0)),
            wspec(wo_a.shape), wspec(wo_b.shape), wspec(g1.shape), wspec(b1.shape),
            wspec(wg.shape), wspec(wu.shape), wspec(wd.shape), wspec(g2.shape), wspec(b2.shape),
        ],
        out_specs=pl.BlockSpec((1, t, d), lambda bi, i: (bi, i, 0)),
        out_shape=jax.ShapeDtypeStruct((b, s, d), F32),
        scratch_shapes=[pltpu.VMEM((t, d_ff), BF16)],
        compiler_params=pltpu.CompilerParams(dimension_semantics=("parallel", "parallel"),
                                             vmem_limit_bytes=VMEM_LIMIT_BYTES),
        name="post_mixer",
    )(x, mix, mem_out, wo_a, wo_b, g1, b1, wg, wu, wd, g2, b2)


def kernel(x, mem, a_w_in, a_ln_v_g, a_ln_v_b, a_w_s, a_b_s, b_w_in, w_mem_kv, w_mix_out,
           ln_mix_g, ln_mix_b, w_ffn_in, w_ffn_out, ln_ffn_g, ln_ffn_b):
    b, s, d = x.shape
    n_mem = mem.shape[1]
    d_ff = w_ffn_out.shape[1]
    assert s % TOKEN_TILE == 0 and TOKEN_TILE % MOBA_BLOCK == 0 and d_ff % FFN_CHUNK == 0
    n_blk = s // MOBA_BLOCK
    assert n_blk % 2 == 0

    kv = _mem_kv(mem.reshape(b * n_mem, d), w_mem_kv).reshape(DEPTH, b, n_mem, 2 * W_MEM)

    def post(i, xin, mix, mem_out):
        wo = w_mix_out[i].astype(BF16)
        wi = w_ffn_in[i].astype(BF16)
        row = lambda v: v.reshape(1, -1)
        return _post_mixer(xin, mix, mem_out, wo[:W_MIX], wo[W_MIX:],
                           row(ln_mix_g[i]), row(ln_mix_b[i]),
                           wi[:, :d_ff], wi[:, d_ff:], w_ffn_out[i].astype(BF16),
                           row(ln_ffn_g[i]), row(ln_ffn_b[i]))

    b_full = jnp.repeat(a_b_s[0].T, HEAD_DIM, axis=1)
    mix, mem_out = _gmlp_front(x, a_w_in[0].astype(BF16), a_ln_v_g[0].reshape(1, -1),
                               a_ln_v_b[0].reshape(1, -1), a_w_s[0].astype(BF16), b_full, kv[0])
    x = post(0, x, mix, mem_out)

    w1 = b_w_in[0].astype(BF16)
    w_qkm = jnp.concatenate([w1[:, :2 * W_MIX], w1[:, 3 * W_MIX:]], axis=1)
    w_vt = w1[:, 2 * W_MIX:3 * W_MIX].T
    q, k, vt, km, mem_out = _moba_front(x, w_qkm, w_vt, kv[1])
    mix = _moba_attn(q, k.reshape(b, n_blk // 2, 2 * MOBA_BLOCK, W_MIX), vt,
                     km.reshape(b, n_blk, W_MIX))
    return post(1, x, mix, mem_out)
```

```python
import jax
import jax.numpy as jnp
from jax import lax
from jax.experimental import pallas as pl
from jax.experimental.pallas import tpu as pltpu

HEAD_DIM = 64
N_MIX_HEADS = 12
W_MIX = N_MIX_HEADS * HEAD_DIM
N_MEM_HEADS = 4
W_MEM = N_MEM_HEADS * HEAD_DIM
GMLP_CHUNK = 128
MOBA_BLOCK = 256
MOBA_TOPK = 3
DEPTH = 2
ALPHA = (2.0 * DEPTH) ** 0.25
LN_EPS = 1e-5
ATTN_SCALE = HEAD_DIM ** -0.5
LOG2_E = 1.4426950408889634
RUNNING_MAX_FLOOR = -1e30
BF16_SUBLANES = 16
V_AUG_ROWS = HEAD_DIM + BF16_SUBLANES

LANES = 128
HEADS_PER_LANE_TILE = LANES // HEAD_DIM
TOKEN_TILE = 512
FFN_CHUNK = 256
VMEM_LIMIT_BYTES = 56 * 1024 * 1024

BF16 = jnp.bfloat16
F32 = jnp.float32

_NT = (((1,), (1,)), ((), ()))


def _layer_norm(y, g, b):
    mu = jnp.mean(y, axis=-1, keepdims=True)
    d = y - mu
    var = jnp.mean(d * d, axis=-1, keepdims=True)
    return d * lax.rsqrt(var + LN_EPS) * g + b


def _head_lane_mask(e):
    lane = lax.broadcasted_iota(jnp.int32, (1, LANES), 1)
    return (lane >= e * HEAD_DIM) & (lane < (e + 1) * HEAD_DIM)


def _mem_cross_attention(qm, kv_ref):
    qb = (qm * ATTN_SCALE).astype(BF16)
    v_all = kv_ref[:, W_MEM:]
    lane_v = lax.broadcasted_iota(jnp.int32, (1, W_MEM), 1)
    out = None
    for hm in range(N_MEM_HEADS):
        tile, e = divmod(hm, HEADS_PER_LANE_TILE)
        qp = qb[:, tile * LANES:(tile + 1) * LANES]
        qz = jnp.where(_head_lane_mask(e), qp, jnp.zeros_like(qp))
        kp = kv_ref[:, tile * LANES:(tile + 1) * LANES]
        s = lax.dot_general(qz, kp, _NT, preferred_element_type=F32)
        m = jnp.max(s, axis=-1, keepdims=True)
        p = jnp.exp(s - m)
        l = jnp.sum(p, axis=-1, keepdims=True)
        pn = (p * (1.0 / l)).astype(BF16)
        vz = jnp.where((lane_v >= hm * HEAD_DIM) & (lane_v < (hm + 1) * HEAD_DIM),
                       v_all, jnp.zeros_like(v_all))
        o = jnp.dot(pn, vz, preferred_element_type=F32)
        out = o if out is None else out + o
    return out


def _mem_kv_kernel(mem_ref, w_ref, o_ref):
    o_ref[0] = jnp.dot(mem_ref[...].astype(BF16), w_ref[0].astype(BF16),
                       preferred_element_type=F32).astype(BF16)


def _mem_kv(mem2d, w_mem_kv):
    rows, d = mem2d.shape
    depth, _, n = w_mem_kv.shape
    return pl.pallas_call(
        _mem_kv_kernel,
        grid=(depth,),
        in_specs=[pl.BlockSpec((rows, d), lambda i: (0, 0)),
                  pl.BlockSpec((1, d, n), lambda i: (i, 0, 0))],
        out_specs=pl.BlockSpec((1, rows, n), lambda i: (i, 0, 0)),
        out_shape=jax.ShapeDtypeStruct((depth, rows, n), BF16),
        compiler_params=pltpu.CompilerParams(dimension_semantics=("parallel",),
                                             vmem_limit_bytes=VMEM_LIMIT_BYTES),
        name="mem_kv",
    )(mem2d, w_mem_kv)


def _gmlp_front_kernel(x_ref, w_ref, lng_ref, lnb_ref, ws_ref, bs_ref, kv_ref,
                       mix_ref, mem_ref):
    t = x_ref.shape[1]
    xb = x_ref[0].astype(BF16)
    u = jax.nn.gelu(jnp.dot(xb, w_ref[:, :W_MIX], preferred_element_type=F32))
    gv = jax.nn.gelu(jnp.dot(xb, w_ref[:, W_MIX:2 * W_MIX], preferred_element_type=F32))
    vb = _layer_norm(gv, lng_ref[...], lnb_ref[...]).astype(BF16)
    qm = jnp.dot(xb, w_ref[:, 2 * W_MIX:], preferred_element_type=F32)

    row = lax.broadcasted_iota(jnp.int32, (GMLP_CHUNK, GMLP_CHUNK), 0)
    col = lax.broadcasted_iota(jnp.int32, (GMLP_CHUNK, GMLP_CHUNK), 1)
    causal = col <= row
    m0 = _head_lane_mask(0)
    m1 = _head_lane_mask(1)
    for p in range(N_MIX_HEADS // HEADS_PER_LANE_TILE):
        w0 = ws_ref[2 * p]
        w1 = ws_ref[2 * p + 1]
        wcat = jnp.concatenate([jnp.where(causal, w0, jnp.zeros_like(w0)),
                                jnp.where(causal, w1, jnp.zeros_like(w1))], axis=1)
        bias = bs_ref[:, p * LANES:(p + 1) * LANES]
        for c in range(t // GMLP_CHUNK):
            rows = slice(c * GMLP_CHUNK, (c + 1) * GMLP_CHUNK)
            vp = vb[rows, p * LANES:(p + 1) * LANES]
            zero = jnp.zeros_like(vp)
            vbd = jnp.concatenate([jnp.where(m0, vp, zero), jnp.where(m1, vp, zero)], axis=0)
            sv = jnp.dot(wcat, vbd, preferred_element_type=F32) + bias
            mix_ref[0, rows, p * LANES:(p + 1) * LANES] = (
                u[rows, p * LANES:(p + 1) * LANES] * sv).astype(BF16)

    mem_ref[0] = _mem_cross_attention(qm, kv_ref.at[0]).astype(BF16)


def _gmlp_front(x, w_in, ln_g, ln_b, w_s, b_full, kv):
    b, s, d = x.shape
    t = TOKEN_TILE
    n_in = w_in.shape[1]
    return pl.pallas_call(
        _gmlp_front_kernel,
        grid=(b, s // t),
        in_specs=[
            pl.BlockSpec((1, t, d), lambda bi, i: (bi, i, 0)),
            pl.BlockSpec((d, n_in), lambda bi, i: (0, 0)),
            pl.BlockSpec((1, W_MIX), lambda bi, i: (0, 0)),
            pl.BlockSpec((1, W_MIX), lambda bi, i: (0, 0)),
            pl.BlockSpec((N_MIX_HEADS, GMLP_CHUNK, GMLP_CHUNK), lambda bi, i: (0, 0, 0)),
            pl.BlockSpec((GMLP_CHUNK, W_MIX), lambda bi, i: (0, 0)),
            pl.BlockSpec((1, kv.shape[1], kv.shape[2]), lambda bi, i: (bi, 0, 0)),
        ],
        out_specs=[pl.BlockSpec((1, t, W_MIX), lambda bi, i: (bi, i, 0)),
                   pl.BlockSpec((1, t, W_MEM), lambda bi, i: (bi, i, 0))],
        out_shape=[jax.ShapeDtypeStruct((b, s, W_MIX), BF16),
                   jax.ShapeDtypeStruct((b, s, W_MEM), BF16)],
        compiler_params=pltpu.CompilerParams(dimension_semantics=("parallel", "parallel"),
                                             vmem_limit_bytes=VMEM_LIMIT_BYTES),
        name="gmlp_front",
    )(x, w_in, ln_g, ln_b, w_s, b_full, kv)


def _moba_front_kernel(x_ref, wqkm_ref, wvt_ref, kv_ref,
                       q_ref, k_ref, vt_ref, km_ref, mem_ref):
    t = x_ref.shape[1]
    xb = x_ref[0].astype(BF16)
    q = jnp.dot(xb, wqkm_ref[:, :W_MIX], preferred_element_type=F32)
    q_ref[0] = (q * (ATTN_SCALE * LOG2_E)).astype(BF16)
    k = jnp.dot(xb, wqkm_ref[:, W_MIX:2 * W_MIX], preferred_element_type=F32)
    k_ref[0] = k.astype(BF16)
    for c in range(t // MOBA_BLOCK):
        km_ref[0, 0, c:c + 1, :] = jnp.mean(k[c * MOBA_BLOCK:(c + 1) * MOBA_BLOCK], axis=0,
                                            keepdims=True)
    vt = lax.dot_general(wvt_ref[...], xb, _NT, preferred_element_type=F32)
    ones = jnp.ones((V_AUG_ROWS - HEAD_DIM, MOBA_BLOCK), BF16)
    for c in range(t // MOBA_BLOCK):
        for h in range(N_MIX_HEADS):
            vt_ref[0, c, h, :HEAD_DIM, :] = vt[h * HEAD_DIM:(h + 1) * HEAD_DIM,
                                               c * MOBA_BLOCK:(c + 1) * MOBA_BLOCK].astype(BF16)
            vt_ref[0, c, h, HEAD_DIM:, :] = ones
    qm = jnp.dot(xb, wqkm_ref[:, 2 * W_MIX:], preferred_element_type=F32)
    mem_ref[0] = _mem_cross_attention(qm, kv_ref.at[0]).astype(BF16)


def _moba_front(x, w_qkm, w_vt, kv):
    b, s, d = x.shape
    t = TOKEN_TILE
    bpt = t // MOBA_BLOCK
    n_blk = s // MOBA_BLOCK
    return pl.pallas_call(
        _moba_front_kernel,
        grid=(b, s // t),
        in_specs=[
            pl.BlockSpec((1, t, d), lambda bi, i: (bi, i, 0)),
            pl.BlockSpec(w_qkm.shape, lambda bi, i: (0, 0)),
            pl.BlockSpec(w_vt.shape, lambda bi, i: (0, 0)),
            pl.BlockSpec((1, kv.shape[1], kv.shape[2]), lambda bi, i: (bi, 0, 0)),
        ],
        out_specs=[
            pl.BlockSpec((1, t, W_MIX), lambda bi, i: (bi, i, 0)),
            pl.BlockSpec((1, t, W_MIX), lambda bi, i: (bi, i, 0)),
            pl.BlockSpec((1, bpt, N_MIX_HEADS, V_AUG_ROWS, MOBA_BLOCK),
                         lambda bi, i: (bi, i, 0, 0, 0)),
            pl.BlockSpec((1, 1, bpt, W_MIX), lambda bi, i: (bi, i, 0, 0)),
            pl.BlockSpec((1, t, W_MEM), lambda bi, i: (bi, i, 0)),
        ],
        out_shape=[
            jax.ShapeDtypeStruct((b, s, W_MIX), BF16),
            jax.ShapeDtypeStruct((b, s, W_MIX), BF16),
            jax.ShapeDtypeStruct((b, n_blk, N_MIX_HEADS, V_AUG_ROWS, MOBA_BLOCK), BF16),
            jax.ShapeDtypeStruct((b, s // t, bpt, W_MIX), F32),
            jax.ShapeDtypeStruct((b, s, W_MEM), BF16),
        ],
        compiler_params=pltpu.CompilerParams(dimension_semantics=("parallel", "parallel"),
                                             vmem_limit_bytes=VMEM_LIMIT_BYTES),
        name="moba_front",
    )(x, w_qkm, w_vt, kv)


def _select_blocks(bs_t, n_past):
    n_blk = bs_t.shape[0]
    nidx = lax.broadcasted_iota(jnp.int32, bs_t.shape, 0)
    rem = nidx < n_past
    sel = jnp.zeros(bs_t.shape, jnp.bool_)
    for _ in range(MOBA_TOPK):
        ms = jnp.where(rem, bs_t, -jnp.inf)
        mx = jnp.max(ms, axis=0, keepdims=True)
        is_max = rem & (ms == mx)
        first = jnp.min(jnp.where(is_max, nidx, n_blk), axis=0, keepdims=True)
        pick = is_max & (nidx == first)
        sel = sel | pick
        rem = rem & jnp.logical_not(pick)
    return sel


def _moba_work_items(n_blk):
    items = []
    for j in range(n_blk):
        own = j // 2
        items.append(j | own << 8 | (j % 2 == 0) << 16 | (j % 2 == 1) << 17 | 1 << 18)
        items.extend(j | i << 8 for i in range(own))
    items.append(items[-1])
    return items


def _moba_attn_kernel(tab_ref, q_ref, k_ref, vt_ref, km_ref, o_ref,
                      qz_ref, bias_ref, cz_ref, s_ref, p_ref, sums_ref):
    nq = MOBA_BLOCK
    heads = range(HEADS_PER_LANE_TILE)
    width = HEADS_PER_LANE_TILE * nq
    n_blk = bias_ref.shape[0]
    n_items = tab_ref.shape[0] - 1

    def head_cols(a, e):
        return a[:, e * nq:(e + 1) * nq]

    def query_rows(tile):
        return pl.ds(pl.multiple_of(tile * nq, nq), nq)

    krow = lax.broadcasted_iota(jnp.int32, (nq, width), 0)
    qcol = lax.broadcasted_iota(jnp.int32, (nq, width), 1) & (nq - 1)
    cz_ref[0] = jnp.zeros((nq, width), F32)
    cz_ref[1] = jnp.where(krow <= qcol, 0.0, -jnp.inf).astype(F32)

    kmb = km_ref[0].astype(BF16)
    nidx = lax.broadcasted_iota(jnp.int32, (n_blk, width), 0)

    def select(j, carry):
        qp = q_ref[0, query_rows(j), :]
        zero = jnp.zeros_like(qp)
        qz = jnp.concatenate([jnp.where(_head_lane_mask(e), qp, zero) for e in heads], axis=0)
        qz_ref[j] = qz
        bs_t = lax.dot_general(kmb, qz, _NT, preferred_element_type=F32)
        keep = _select_blocks(bs_t, j) | (nidx == j)
        bias_ref[j] = jnp.where(keep, 0.0, -jnp.inf).astype(F32)
        return carry

    lax.fori_loop(0, n_blk, select, 0, unroll=4)

    def decode(w):
        code = tab_ref[w]
        return code & 0xFF, (code >> 8) & 0xFF, (code >> 16) & 1, (code >> 17) & 1, (code >> 18) & 1

    def block_bias(w):
        tile, pair = decode(w)[:2]
        return bias_ref[tile, pl.ds(2 * pair, 1), :], bias_ref[tile, pl.ds(2 * pair + 1, 1), :]

    def masked_scores(w, s_buf):
        tile, pair, causal_lo, causal_hi, _ = decode(w)
        s = lax.dot_general(k_ref[0, pair], qz_ref[tile], _NT, preferred_element_type=F32)
        s0 = s[:nq] + cz_ref[causal_lo]
        s1 = s[nq:] + cz_ref[causal_hi]
        s_buf[:nq] = s0
        s_buf[nq:] = s1
        b0, b1 = block_bias(w)
        return jnp.maximum(jnp.max(s0, axis=0, keepdims=True) + b0,
                           jnp.max(s1, axis=0, keepdims=True) + b1)

    def probabilities(w, m, cmax, s_buf):
        first = decode(w)[4] == 1
        m = jnp.where(first, jnp.full_like(m, RUNNING_MAX_FLOOR), m)
        m_new = jnp.maximum(m, cmax)
        b0, b1 = block_bias(w)
        p0 = jnp.exp2(s_buf[:nq] - (m_new - b0)).astype(BF16)
        p1 = jnp.exp2(s_buf[nq:] - (m_new - b1)).astype(BF16)
        for e in heads:
            p_ref[e, :nq] = head_cols(p0, e)
            p_ref[e, nq:] = head_cols(p1, e)
        return m_new, jnp.where(first, jnp.zeros_like(m), jnp.exp2(m - m_new))

    def accumulate(w, alpha, acc):
        tile, pair = decode(w)[:2]
        acc = tuple(
            head_cols(alpha, e) * acc[e]
            + (jnp.dot(vt_ref[0, 2 * pair, e], p_ref[e, :nq], preferred_element_type=F32)
               + jnp.dot(vt_ref[0, 2 * pair + 1, e], p_ref[e, nq:], preferred_element_type=F32))
            for e in heads)
        for e in heads:
            sums_ref[tile, e] = acc[e]
        return acc

    def one_item(w, carry):
        m, acc, cmax, alpha = carry
        acc = accumulate(jnp.maximum(w - 1, 0), alpha, acc)
        m, alpha = probabilities(w, m, cmax, s_ref)
        return m, acc, masked_scores(w + 1, s_ref), alpha

    p_ref[...] = jnp.zeros(p_ref.shape, BF16)
    carry = (jnp.zeros((1, width), F32), tuple(jnp.ones((V_AUG_ROWS, nq), F32) for _ in heads),
             masked_scores(0, s_ref), jnp.ones((1, width), F32))
    m, acc, _, alpha = lax.fori_loop(0, n_items, one_item, carry, unroll=2)
    accumulate(n_items - 1, alpha, acc)

    def normalise(j, carry):
        o_t = jnp.concatenate(
            [sums_ref[j, e, :HEAD_DIM] / sums_ref[j, e, HEAD_DIM:HEAD_DIM + 1] for e in heads],
            axis=0)
        o_ref[0, query_rows(j), :] = o_t.T.astype(BF16)
        return carry

    lax.fori_loop(0, n_blk, normalise, 0, unroll=4)


def _moba_attn(q, k_pairs, vt, km):
    b, s, _ = q.shape
    n_blk = s // MOBA_BLOCK
    n_tiles = W_MIX // LANES
    width = HEADS_PER_LANE_TILE * MOBA_BLOCK
    table = jnp.asarray(_moba_work_items(n_blk), jnp.int32)
    grid_spec = pltpu.PrefetchScalarGridSpec(
        num_scalar_prefetch=1,
        grid=(b, n_tiles),
        in_specs=[
            pl.BlockSpec((1, s, LANES), lambda bi, p, tab: (bi, 0, p)),
            pl.BlockSpec((1, n_blk // 2, 2 * MOBA_BLOCK, LANES), lambda bi, p, tab: (bi, 0, 0, p)),
            pl.BlockSpec((1, n_blk, HEADS_PER_LANE_TILE, V_AUG_ROWS, MOBA_BLOCK),
                         lambda bi, p, tab: (bi, 0, p, 0, 0)),
            pl.BlockSpec((1, n_blk, LANES), lambda bi, p, tab: (bi, 0, p)),
        ],
        out_specs=pl.BlockSpec((1, s, LANES), lambda bi, p, tab: (bi, 0, p)),
        scratch_shapes=[
            pltpu.VMEM((n_blk, width, LANES), BF16),
            pltpu.VMEM((n_blk, n_blk, width), F32),
            pltpu.VMEM((2, MOBA_BLOCK, width), F32),
            pltpu.VMEM((2 * MOBA_BLOCK, width), F32),
            pltpu.VMEM((HEADS_PER_LANE_TILE, 2 * MOBA_BLOCK, MOBA_BLOCK), BF16),
            pltpu.VMEM((n_blk, HEADS_PER_LANE_TILE, V_AUG_ROWS, MOBA_BLOCK), F32),
        ],
    )
    return pl.pallas_call(
        _moba_attn_kernel,
        grid_spec=grid_spec,
        out_shape=jax.ShapeDtypeStruct((b, s, W_MIX), BF16),
        compiler_params=pltpu.CompilerParams(
            dimension_semantics=("parallel", "parallel"),
            vmem_limit_bytes=VMEM_LIMIT_BYTES),
        name="moba_attn",
    )(table, q, k_pairs, vt, km)


def _post_mixer_kernel(x_ref, mix_ref, mem_ref, woa_ref, wob_ref, g1_ref, b1_ref,
                       wg_ref, wu_ref, wd_ref, g2_ref, b2_ref, o_ref, act_ref):
    sub = (jnp.dot(mix_ref[0], woa_ref[...], preferred_element_type=F32)
           + jnp.dot(mem_ref[0], wob_ref[...], preferred_element_type=F32))
    x1 = _layer_norm(ALPHA * x_ref[0] + sub, g1_ref[...], b1_ref[...])
    x1b = x1.astype(BF16)
    d_ff = wg_ref.shape[1]
    for c in range(d_ff // FFN_CHUNK):
        cols = slice(c * FFN_CHUNK, (c + 1) * FFN_CHUNK)
        g = jnp.dot(x1b, wg_ref[:, cols], preferred_element_type=F32)
        u = jnp.dot(x1b, wu_ref[:, cols], preferred_element_type=F32)
        act_ref[:, cols] = (jax.nn.silu(g) * u).astype(BF16)
    y = jnp.dot(act_ref[...], wd_ref[...], preferred_element_type=F32)
    o_ref[0] = _layer_norm(ALPHA * x1 + y, g2_ref[...], b2_ref[...])


def _post_mixer(x, mix, mem_out, wo_a, wo_b, g1, b1, wg, wu, wd, g2, b2):
    b, s, d = x.shape
    t = TOKEN_TILE
    d_ff = wg.shape[1]
    const = lambda bi, i: (0, 0)
    single = pl.Buffered(1)

    def wspec(shape):
        return pl.BlockSpec(shape, const, pipeline_mode=single)

    return pl.pallas_call(
        _post_mixer_kernel,
        grid=(b, s // t),
        in_specs=[
            pl.BlockSpec((1, t, d), lambda bi, i: (bi, i, 0)),
            pl.BlockSpec((1, t, W_MIX), lambda bi, i: (bi, i, 0)),
            pl.BlockSpec((1, t, W_MEM), lambda bi, i: (bi, i, 0)),
            wspec(wo_a.shape), wspec(wo_b.shape), wspec(g1.shape), wspec(b1.shape),
            wspec(wg.shape), wspec(wu.shape), wspec(wd.shape), wspec(g2.shape), wspec(b2.shape),
        ],
        out_specs=pl.BlockSpec((1, t, d), lambda bi, i: (bi, i, 0)),
        out_shape=jax.ShapeDtypeStruct((b, s, d), F32),
        scratch_shapes=[pltpu.VMEM((t, d_ff), BF16)],
        compiler_params=pltpu.CompilerParams(dimension_semantics=("parallel", "parallel"),
                                             vmem_limit_bytes=VMEM_LIMIT_BYTES),
        name="post_mixer",
    )(x, mix, mem_out, wo_a, wo_b, g1, b1, wg, wu, wd, g2, b2)


def kernel(x, mem, a_w_in, a_ln_v_g, a_ln_v_b, a_w_s, a_b_s, b_w_in, w_mem_kv, w_mix_out,
           ln_mix_g, ln_mix_b, w_ffn_in, w_ffn_out, ln_ffn_g, ln_ffn_b):
    b, s, d = x.shape
    n_mem = mem.shape[1]
    d_ff = w_ffn_out.shape[1]
    assert s % TOKEN_TILE == 0 and TOKEN_TILE % MOBA_BLOCK == 0 and d_ff % FFN_CHUNK == 0
    n_blk = s // MOBA_BLOCK
    assert n_blk % 2 == 0

    kv = _mem_kv(mem.reshape(b * n_mem, d), w_mem_kv).reshape(DEPTH, b, n_mem, 2 * W_MEM)

    def post(i, xin, mix, mem_out):
        wo = w_mix_out[i].astype(BF16)
        wi = w_ffn_in[i].astype(BF16)
        row = lambda v: v.reshape(1, -1)
        return _post_mixer(xin, mix, mem_out, wo[:W_MIX], wo[W_MIX:],
                           row(ln_mix_g[i]), row(ln_mix_b[i]),
                           wi[:, :d_ff], wi[:, d_ff:], w_ffn_out[i].astype(BF16),
                           row(ln_ffn_g[i]), row(ln_ffn_b[i]))

    b_full = jnp.repeat(a_b_s[0].T, HEAD_DIM, axis=1)
    mix, mem_out = _gmlp_front(x, a_w_in[0].astype(BF16), a_ln_v_g[0].reshape(1, -1),
                               a_ln_v_b[0].reshape(1, -1), a_w_s[0].astype(BF16), b_full, kv[0])
    x = post(0, x, mix, mem_out)

    w1 = b_w_in[0].astype(BF16)
    w_qkm = jnp.concatenate([w1[:, :2 * W_MIX], w1[:, 3 * W_MIX:]], axis=1)
    w_vt = w1[:, 2 * W_MIX:3 * W_MIX].T
    q, k, vt, km, mem_out = _moba_front(x, w_qkm, w_vt, kv[1])
    mix = _moba_attn(q, k.reshape(b, n_blk // 2, 2 * MOBA_BLOCK, W_MIX), vt,
                     km.reshape(b, n_blk, W_MIX))
    return post(1, x, mix, mem_out)
```

```python
import jax
import jax.numpy as jnp
from jax import lax
from jax.experimental import pallas as pl
from jax.experimental.pallas import tpu as pltpu

HEAD_DIM = 64
N_MIX_HEADS = 12
W_MIX = N_MIX_HEADS * HEAD_DIM
N_MEM_HEADS = 4
W_MEM = N_MEM_HEADS * HEAD_DIM
GMLP_CHUNK = 128
MOBA_BLOCK = 256
MOBA_TOPK = 3
DEPTH = 2
ALPHA = (2.0 * DEPTH) ** 0.25
LN_EPS = 1e-5
ATTN_SCALE = HEAD_DIM ** -0.5
LOG2_E = 1.4426950408889634
RUNNING_MAX_FLOOR = -1e30
BF16_SUBLANES = 16
V_AUG_ROWS = HEAD_DIM + BF16_SUBLANES

LANES = 128
HEADS_PER_LANE_TILE = LANES // HEAD_DIM
TOKEN_TILE = 512
FRONT_TILE = 1024
SUB_TILE = 512
FFN_CHUNK = 256
VMEM_LIMIT_BYTES = 56 * 1024 * 1024

BF16 = jnp.bfloat16
F32 = jnp.float32

_NT = (((1,), (1,)), ((), ()))


def _layer_norm(y, g, b):
    mu = jnp.mean(y, axis=-1, keepdims=True)
    d = y - mu
    var = jnp.mean(d * d, axis=-1, keepdims=True)
    return d * lax.rsqrt(var + LN_EPS) * g + b


def _head_lane_mask(e):
    lane = lax.broadcasted_iota(jnp.int32, (1, LANES), 1)
    return (lane >= e * HEAD_DIM) & (lane < (e + 1) * HEAD_DIM)


def _mem_cross_attention(qm, kv_ref):
    qb = (qm * ATTN_SCALE).astype(BF16)
    v_all = kv_ref[:, W_MEM:]
    lane_v = lax.broadcasted_iota(jnp.int32, (1, W_MEM), 1)
    out = None
    for hm in range(N_MEM_HEADS):
        tile, e = divmod(hm, HEADS_PER_LANE_TILE)
        qp = qb[:, tile * LANES:(tile + 1) * LANES]
        qz = jnp.where(_head_lane_mask(e), qp, jnp.zeros_like(qp))
        kp = kv_ref[:, tile * LANES:(tile + 1) * LANES]
        s = lax.dot_general(qz, kp, _NT, preferred_element_type=F32)
        m = jnp.max(s, axis=-1, keepdims=True)
        p = jnp.exp(s - m)
        l = jnp.sum(p, axis=-1, keepdims=True)
        pn = (p * (1.0 / l)).astype(BF16)
        vz = jnp.where((lane_v >= hm * HEAD_DIM) & (lane_v < (hm + 1) * HEAD_DIM),
                       v_all, jnp.zeros_like(v_all))
        o = jnp.dot(pn, vz, preferred_element_type=F32)
        out = o if out is None else out + o
    return out


def _mem_kv_kernel(mem_ref, w_ref, o_ref):
    o_ref[0] = jnp.dot(mem_ref[...].astype(BF16), w_ref[0].astype(BF16),
                       preferred_element_type=F32).astype(BF16)


def _mem_kv(mem2d, w_mem_kv):
    rows, d = mem2d.shape
    depth, _, n = w_mem_kv.shape
    return pl.pallas_call(
        _mem_kv_kernel,
        grid=(depth,),
        in_specs=[pl.BlockSpec((rows, d), lambda i: (0, 0)),
                  pl.BlockSpec((1, d, n), lambda i: (i, 0, 0))],
        out_specs=pl.BlockSpec((1, rows, n), lambda i: (i, 0, 0)),
        out_shape=jax.ShapeDtypeStruct((depth, rows, n), BF16),
        compiler_params=pltpu.CompilerParams(dimension_semantics=("parallel",),
                                             vmem_limit_bytes=VMEM_LIMIT_BYTES),
        name="mem_kv",
    )(mem2d, w_mem_kv)


def _gmlp_front_kernel(x_ref, w_ref, lng_ref, lnb_ref, ws_ref, bs_ref, kv_ref,
                       mix_ref, mem_ref):
    row = lax.broadcasted_iota(jnp.int32, (GMLP_CHUNK, GMLP_CHUNK), 0)
    col = lax.broadcasted_iota(jnp.int32, (GMLP_CHUNK, GMLP_CHUNK), 1)
    causal = col <= row
    m0 = _head_lane_mask(0)
    m1 = _head_lane_mask(1)
    wcat = []
    for p in range(N_MIX_HEADS // HEADS_PER_LANE_TILE):
        w0 = ws_ref[2 * p]
        w1 = ws_ref[2 * p + 1]
        wcat.append(jnp.concatenate([jnp.where(causal, w0, jnp.zeros_like(w0)),
                                     jnp.where(causal, w1, jnp.zeros_like(w1))], axis=1))

    def sub_tile(r0):
        xb = x_ref[0, r0:r0 + SUB_TILE, :].astype(BF16)
        u = jax.nn.gelu(jnp.dot(xb, w_ref[:, :W_MIX], preferred_element_type=F32))
        gv = jax.nn.gelu(jnp.dot(xb, w_ref[:, W_MIX:2 * W_MIX], preferred_element_type=F32))
        vb = _layer_norm(gv, lng_ref[...], lnb_ref[...]).astype(BF16)
        qm = jnp.dot(xb, w_ref[:, 2 * W_MIX:], preferred_element_type=F32)
        for p in range(N_MIX_HEADS // HEADS_PER_LANE_TILE):
            lanes = slice(p * LANES, (p + 1) * LANES)
            bias = bs_ref[:, lanes]
            for c in range(SUB_TILE // GMLP_CHUNK):
                rows = slice(c * GMLP_CHUNK, (c + 1) * GMLP_CHUNK)
                vp = vb[rows, lanes]
                zero = jnp.zeros_like(vp)
                vbd = jnp.concatenate([jnp.where(m0, vp, zero), jnp.where(m1, vp, zero)], axis=0)
                sv = jnp.dot(wcat[p], vbd, preferred_element_type=F32) + bias
                mix_ref[0, r0 + c * GMLP_CHUNK:r0 + (c + 1) * GMLP_CHUNK, lanes] = (
                    u[rows, lanes] * sv).astype(BF16)
        mem_ref[0, r0:r0 + SUB_TILE, :] = _mem_cross_attention(qm, kv_ref.at[0]).astype(BF16)

    for r0 in range(0, x_ref.shape[1], SUB_TILE):
        sub_tile(r0)


def _gmlp_front(x, w_in, ln_g, ln_b, w_s, b_full, kv):
    b, s, d = x.shape
    t = FRONT_TILE
    n_in = w_in.shape[1]
    return pl.pallas_call(
        _gmlp_front_kernel,
        grid=(b, s // t),
        in_specs=[
            pl.BlockSpec((1, t, d), lambda bi, i: (bi, i, 0)),
            pl.BlockSpec((d, n_in), lambda bi, i: (0, 0)),
            pl.BlockSpec((1, W_MIX), lambda bi, i: (0, 0)),
            pl.BlockSpec((1, W_MIX), lambda bi, i: (0, 0)),
            pl.BlockSpec((N_MIX_HEADS, GMLP_CHUNK, GMLP_CHUNK), lambda bi, i: (0, 0, 0)),
            pl.BlockSpec((GMLP_CHUNK, W_MIX), lambda bi, i: (0, 0)),
            pl.BlockSpec((1, kv.shape[1], kv.shape[2]), lambda bi, i: (bi, 0, 0)),
        ],
        out_specs=[pl.BlockSpec((1, t, W_MIX), lambda bi, i: (bi, i, 0)),
                   pl.BlockSpec((1, t, W_MEM), lambda bi, i: (bi, i, 0))],
        out_shape=[jax.ShapeDtypeStruct((b, s, W_MIX), BF16),
                   jax.ShapeDtypeStruct((b, s, W_MEM), BF16)],
        compiler_params=pltpu.CompilerParams(dimension_semantics=("parallel", "parallel"),
                                             vmem_limit_bytes=VMEM_LIMIT_BYTES),
        name="gmlp_front",
    )(x, w_in, ln_g, ln_b, w_s, b_full, kv)


def _moba_front_kernel(x_ref, wqkm_ref, wvt_ref, kv_ref,
                       q_ref, k_ref, vt_ref, km_ref, mem_ref):
    ones = jnp.ones((V_AUG_ROWS - HEAD_DIM, MOBA_BLOCK), BF16)

    def sub_tile(r0):
        rows = slice(r0, r0 + SUB_TILE)
        xb = x_ref[0, rows, :].astype(BF16)
        q = jnp.dot(xb, wqkm_ref[:, :W_MIX], preferred_element_type=F32)
        q_ref[0, rows, :] = (q * (ATTN_SCALE * LOG2_E)).astype(BF16)
        k = jnp.dot(xb, wqkm_ref[:, W_MIX:2 * W_MIX], preferred_element_type=F32)
        k_ref[0, rows, :] = k.astype(BF16)
        vt = lax.dot_general(wvt_ref[...], xb, _NT, preferred_element_type=F32)
        for c in range(SUB_TILE // MOBA_BLOCK):
            blk = r0 // MOBA_BLOCK + c
            cols = slice(c * MOBA_BLOCK, (c + 1) * MOBA_BLOCK)
            km_ref[0, 0, blk:blk + 1, :] = jnp.mean(k[cols], axis=0, keepdims=True)
            for h in range(N_MIX_HEADS):
                vt_ref[0, blk, h, :HEAD_DIM, :] = vt[h * HEAD_DIM:(h + 1) * HEAD_DIM,
                                                     cols].astype(BF16)
                vt_ref[0, blk, h, HEAD_DIM:, :] = ones
        qm = jnp.dot(xb, wqkm_ref[:, 2 * W_MIX:], preferred_element_type=F32)
        mem_ref[0, rows, :] = _mem_cross_attention(qm, kv_ref.at[0]).astype(BF16)

    for r0 in range(0, x_ref.shape[1], SUB_TILE):
        sub_tile(r0)


def _moba_front(x, w_qkm, w_vt, kv):
    b, s, d = x.shape
    t = FRONT_TILE
    bpt = t // MOBA_BLOCK
    n_blk = s // MOBA_BLOCK
    return pl.pallas_call(
        _moba_front_kernel,
        grid=(b, s // t),
        in_specs=[
            pl.BlockSpec((1, t, d), lambda bi, i: (bi, i, 0)),
            pl.BlockSpec(w_qkm.shape, lambda bi, i: (0, 0)),
            pl.BlockSpec(w_vt.shape, lambda bi, i: (0, 0)),
            pl.BlockSpec((1, kv.shape[1], kv.shape[2]), lambda bi, i: (bi, 0, 0)),
        ],
        out_specs=[
            pl.BlockSpec((1, t, W_MIX), lambda bi, i: (bi, i, 0)),
            pl.BlockSpec((1, t, W_MIX), lambda bi, i: (bi, i, 0)),
            pl.BlockSpec((1, bpt, N_MIX_HEADS, V_AUG_ROWS, MOBA_BLOCK),
                         lambda bi, i: (bi, i, 0, 0, 0)),
            pl.BlockSpec((1, 1, bpt, W_MIX), lambda bi, i: (bi, i, 0, 0)),
            pl.BlockSpec((1, t, W_MEM), lambda bi, i: (bi, i, 0)),
        ],
        out_shape=[
            jax.ShapeDtypeStruct((b, s, W_MIX), BF16),
            jax.ShapeDtypeStruct((b, s, W_MIX), BF16),
            jax.ShapeDtypeStruct((b, n_blk, N_MIX_HEADS, V_AUG_ROWS, MOBA_BLOCK), BF16),
            jax.ShapeDtypeStruct((b, s // t, bpt, W_MIX), F32),
            jax.ShapeDtypeStruct((b, s, W_MEM), BF16),
        ],
        compiler_params=pltpu.CompilerParams(dimension_semantics=("parallel", "parallel"),
                                             vmem_limit_bytes=VMEM_LIMIT_BYTES),
        name="moba_front",
    )(x, w_qkm, w_vt, kv)


def _select_blocks(bs_t, n_past):
    n_blk = bs_t.shape[0]
    nidx = lax.broadcasted_iota(jnp.int32, bs_t.shape, 0)
    rem = nidx < n_past
    sel = jnp.zeros(bs_t.shape, jnp.bool_)
    for _ in range(MOBA_TOPK):
        ms = jnp.where(rem, bs_t, -jnp.inf)
        mx = jnp.max(ms, axis=0, keepdims=True)
        is_max = rem & (ms == mx)
        first = jnp.min(jnp.where(is_max, nidx, n_blk), axis=0, keepdims=True)
        pick = is_max & (nidx == first)
        sel = sel | pick
        rem = rem & jnp.logical_not(pick)
    return sel


def _moba_work_items(n_blk):
    items = []
    for j in range(n_blk):
        own = j // 2
        items.append(j | own << 8 | (j % 2 == 0) << 16 | (j % 2 == 1) << 17 | 1 << 18)
        items.extend(j | i << 8 for i in range(own))
    items.append(items[-1])
    return items


def _moba_attn_kernel(tab_ref, q_ref, k_ref, vt_ref, km_ref, o_ref,
                      qz_ref, bias_ref, cz_ref, s_ref, p_ref, sums_ref):
    nq = MOBA_BLOCK
    heads = range(HEADS_PER_LANE_TILE)
    width = HEADS_PER_LANE_TILE * nq
    n_blk = bias_ref.shape[0]
    n_items = tab_ref.shape[0] - 1

    def head_cols(a, e):
        return a[:, e * nq:(e + 1) * nq]

    def query_rows(tile):
        return pl.ds(pl.multiple_of(tile * nq, nq), nq)

    krow = lax.broadcasted_iota(jnp.int32, (nq, width), 0)
    qcol = lax.broadcasted_iota(jnp.int32, (nq, width), 1) & (nq - 1)
    cz_ref[0] = jnp.zeros((nq, width), F32)
    cz_ref[1] = jnp.where(krow <= qcol, 0.0, -jnp.inf).astype(F32)

    kmb = km_ref[0].astype(BF16)
    nidx = lax.broadcasted_iota(jnp.int32, (n_blk, width), 0)

    def select(j, carry):
        qp = q_ref[0, query_rows(j), :]
        zero = jnp.zeros_like(qp)
        qz = jnp.concatenate([jnp.where(_head_lane_mask(e), qp, zero) for e in heads], axis=0)
        qz_ref[j] = qz
        bs_t = lax.dot_general(kmb, qz, _NT, preferred_element_type=F32)
        keep = _select_blocks(bs_t, j) | (nidx == j)
        bias_ref[j] = jnp.where(keep, 0.0, -jnp.inf).astype(F32)
        return carry

    lax.fori_loop(0, n_blk, select, 0, unroll=4)

    def decode(w):
        code = tab_ref[w]
        return code & 0xFF, (code >> 8) & 0xFF, (code >> 16) & 1, (code >> 17) & 1, (code >> 18) & 1

    def block_bias(w):
        tile, pair = decode(w)[:2]
        return bias_ref[tile, pl.ds(2 * pair, 1), :], bias_ref[tile, pl.ds(2 * pair + 1, 1), :]

    def masked_scores(w, s_buf):
        tile, pair, causal_lo, causal_hi, _ = decode(w)
        s = lax.dot_general(k_ref[0, pair], qz_ref[tile], _NT, preferred_element_type=F32)
        s0 = s[:nq] + cz_ref[causal_lo]
        s1 = s[nq:] + cz_ref[causal_hi]
        s_buf[:nq] = s0
        s_buf[nq:] = s1
        b0, b1 = block_bias(w)
        return jnp.maximum(jnp.max(s0, axis=0, keepdims=True) + b0,
                           jnp.max(s1, axis=0, keepdims=True) + b1)

    def probabilities(w, m, cmax, s_buf):
        first = decode(w)[4] == 1
        m = jnp.where(first, jnp.full_like(m, RUNNING_MAX_FLOOR), m)
        m_new = jnp.maximum(m, cmax)
        b0, b1 = block_bias(w)
        p0 = jnp.exp2(s_buf[:nq] - (m_new - b0)).astype(BF16)
        p1 = jnp.exp2(s_buf[nq:] - (m_new - b1)).astype(BF16)
        for e in heads:
            p_ref[e, :nq] = head_cols(p0, e)
            p_ref[e, nq:] = head_cols(p1, e)
        return m_new, jnp.where(first, jnp.zeros_like(m), jnp.exp2(m - m_new))

    def accumulate(w, alpha, acc):
        tile, pair = decode(w)[:2]
        acc = tuple(
            head_cols(alpha, e) * acc[e]
            + (jnp.dot(vt_ref[0, 2 * pair, e], p_ref[e, :nq], preferred_element_type=F32)
               + jnp.dot(vt_ref[0, 2 * pair + 1, e], p_ref[e, nq:], preferred_element_type=F32))
            for e in heads)
        for e in heads:
            sums_ref[tile, e] = acc[e]
        return acc

    def one_item(w, carry):
        m, acc, cmax, alpha = carry
        acc = accumulate(jnp.maximum(w - 1, 0), alpha, acc)
        m, alpha = probabilities(w, m, cmax, s_ref)
        return m, acc, masked_scores(w + 1, s_ref), alpha

    p_ref[...] = jnp.zeros(p_ref.shape, BF16)
    carry = (jnp.zeros((1, width), F32), tuple(jnp.ones((V_AUG_ROWS, nq), F32) for _ in heads),
             masked_scores(0, s_ref), jnp.ones((1, width), F32))
    m, acc, _, alpha = lax.fori_loop(0, n_items, one_item, carry, unroll=2)
    accumulate(n_items - 1, alpha, acc)

    def normalise(j, carry):
        o_t = jnp.concatenate(
            [sums_ref[j, e, :HEAD_DIM] / sums_ref[j, e, HEAD_DIM:HEAD_DIM + 1] for e in heads],
            axis=0)
        o_ref[0, query_rows(j), :] = o_t.T.astype(BF16)
        return carry

    lax.fori_loop(0, n_blk, normalise, 0, unroll=4)


def _moba_attn(q, k_pairs, vt, km):
    b, s, _ = q.shape
    n_blk = s // MOBA_BLOCK
    n_tiles = W_MIX // LANES
    width = HEADS_PER_LANE_TILE * MOBA_BLOCK
    table = jnp.asarray(_moba_work_items(n_blk), jnp.int32)
    grid_spec = pltpu.PrefetchScalarGridSpec(
        num_scalar_prefetch=1,
        grid=(b, n_tiles),
        in_specs=[
            pl.BlockSpec((1, s, LANES), lambda bi, p, tab: (bi, 0, p)),
            pl.BlockSpec((1, n_blk // 2, 2 * MOBA_BLOCK, LANES), lambda bi, p, tab: (bi, 0, 0, p)),
            pl.BlockSpec((1, n_blk, HEADS_PER_LANE_TILE, V_AUG_ROWS, MOBA_BLOCK),
                         lambda bi, p, tab: (bi, 0, p, 0, 0)),
            pl.BlockSpec((1, n_blk, LANES), lambda bi, p, tab: (bi, 0, p)),
        ],
        out_specs=pl.BlockSpec((1, s, LANES), lambda bi, p, tab: (bi, 0, p)),
        scratch_shapes=[
            pltpu.VMEM((n_blk, width, LANES), BF16),
            pltpu.VMEM((n_blk, n_blk, width), F32),
            pltpu.VMEM((2, MOBA_BLOCK, width), F32),
            pltpu.VMEM((2 * MOBA_BLOCK, width), F32),
            pltpu.VMEM((HEADS_PER_LANE_TILE, 2 * MOBA_BLOCK, MOBA_BLOCK), BF16),
            pltpu.VMEM((n_blk, HEADS_PER_LANE_TILE, V_AUG_ROWS, MOBA_BLOCK), F32),
        ],
    )
    return pl.pallas_call(
        _moba_attn_kernel,
        grid_spec=grid_spec,
        out_shape=jax.ShapeDtypeStruct((b, s, W_MIX), BF16),
        compiler_params=pltpu.CompilerParams(
            dimension_semantics=("parallel", "parallel"),
            vmem_limit_bytes=VMEM_LIMIT_BYTES),
        name="moba_attn",
    )(table, q, k_pairs, vt, km)


def _post_mixer_kernel(x_ref, mix_ref, mem_ref, wo_ref, g1_ref, b1_ref,
                       wi_ref, wd_ref, g2_ref, b2_ref, o_ref, act_ref):
    sub = (jnp.dot(mix_ref[0], wo_ref[0, :W_MIX, :], preferred_element_type=F32)
           + jnp.dot(mem_ref[0], wo_ref[0, W_MIX:, :], preferred_element_type=F32))
    x1 = _layer_norm(ALPHA * x_ref[0] + sub, g1_ref[0], b1_ref[0])
    x1b = x1.astype(BF16)
    d_ff = wd_ref.shape[1]
    for c in range(d_ff // FFN_CHUNK):
        cols = slice(c * FFN_CHUNK, (c + 1) * FFN_CHUNK)
        up_cols = slice(d_ff + c * FFN_CHUNK, d_ff + (c + 1) * FFN_CHUNK)
        g = jnp.dot(x1b, wi_ref[0, :, cols], preferred_element_type=F32)
        u = jnp.dot(x1b, wi_ref[0, :, up_cols], preferred_element_type=F32)
        act_ref[:, cols] = (jax.nn.silu(g) * u).astype(BF16)
    y = jnp.dot(act_ref[...], wd_ref[0], preferred_element_type=F32)
    o_ref[0] = _layer_norm(ALPHA * x1 + y, g2_ref[0], b2_ref[0])


def _post_mixer(layer, x, mix, mem_out, wo, g1, b1, wi, wd, g2, b2):
    b, s, d = x.shape
    t = TOKEN_TILE
    d_ff = wd.shape[1]
    single = pl.Buffered(1)

    def wspec(arr):
        return pl.BlockSpec((1,) + arr.shape[1:], lambda bi, i: (layer,) + (0,) * (arr.ndim - 1),
                            pipeline_mode=single)

    return pl.pallas_call(
        _post_mixer_kernel,
        grid=(b, s // t),
        in_specs=[
            pl.BlockSpec((1, t, d), lambda bi, i: (bi, i, 0)),
            pl.BlockSpec((1, t, W_MIX), lambda bi, i: (bi, i, 0)),
            pl.BlockSpec((1, t, W_MEM), lambda bi, i: (bi, i, 0)),
            wspec(wo), wspec(g1), wspec(b1), wspec(wi), wspec(wd), wspec(g2), wspec(b2),
        ],
        out_specs=pl.BlockSpec((1, t, d), lambda bi, i: (bi, i, 0)),
        out_shape=jax.ShapeDtypeStruct((b, s, d), F32),
        scratch_shapes=[pltpu.VMEM((t, d_ff), BF16)],
        compiler_params=pltpu.CompilerParams(dimension_semantics=("parallel", "parallel"),
                                             vmem_limit_bytes=VMEM_LIMIT_BYTES),
        name="post_mixer",
    )(x, mix, mem_out, wo, g1, b1, wi, wd, g2, b2)


def kernel(x, mem, a_w_in, a_ln_v_g, a_ln_v_b, a_w_s, a_b_s, b_w_in, w_mem_kv, w_mix_out,
           ln_mix_g, ln_mix_b, w_ffn_in, w_ffn_out, ln_ffn_g, ln_ffn_b):
    b, s, d = x.shape
    n_mem = mem.shape[1]
    d_ff = w_ffn_out.shape[1]
    assert s % FRONT_TILE == 0 and FRONT_TILE % SUB_TILE == 0 and SUB_TILE % MOBA_BLOCK == 0
    assert s % TOKEN_TILE == 0 and d_ff % FFN_CHUNK == 0
    n_blk = s // MOBA_BLOCK
    assert n_blk % 2 == 0

    kv = _mem_kv(mem.reshape(b * n_mem, d), w_mem_kv).reshape(DEPTH, b, n_mem, 2 * W_MEM)

    wo, wi, wd = (w.astype(BF16) for w in (w_mix_out, w_ffn_in, w_ffn_out))
    ln_rows = [v.reshape(DEPTH, 1, d) for v in (ln_mix_g, ln_mix_b, ln_ffn_g, ln_ffn_b)]

    def post(i, xin, mix, mem_out):
        return _post_mixer(i, xin, mix, mem_out, wo, ln_rows[0], ln_rows[1], wi, wd,
                           ln_rows[2], ln_rows[3])

    b_full = jnp.repeat(a_b_s[0].T, HEAD_DIM, axis=1)
    mix, mem_out = _gmlp_front(x, a_w_in[0].astype(BF16), a_ln_v_g[0].reshape(1, -1),
                               a_ln_v_b[0].reshape(1, -1), a_w_s[0].astype(BF16), b_full, kv[0])
    x = post(0, x, mix, mem_out)

    w1 = b_w_in[0].astype(BF16)
    w_qkm = jnp.concatenate([w1[:, :2 * W_MIX], w1[:, 3 * W_MIX:]], axis=1)
    w_vt = w1[:, 2 * W_MIX:3 * W_MIX].T
    q, k, vt, km, mem_out = _moba_front(x, w_qkm, w_vt, kv[1])
    mix = _moba_attn(q, k.reshape(b, n_blk // 2, 2 * MOBA_BLOCK, W_MIX), vt,
                     km.reshape(b, n_blk, W_MIX))
    return post(1, x, mix, mem_out)
```

```python
import jax
import jax.numpy as jnp
from jax import lax
from jax.experimental import pallas as pl
from jax.experimental.pallas import tpu as pltpu

HEAD_DIM = 64
N_MIX_HEADS = 12
W_MIX = N_MIX_HEADS * HEAD_DIM
N_MEM_HEADS = 4
W_MEM = N_MEM_HEADS * HEAD_DIM
GMLP_CHUNK = 128
MOBA_BLOCK = 256
MOBA_TOPK = 3
DEPTH = 2
ALPHA = (2.0 * DEPTH) ** 0.25
LN_EPS = 1e-5
ATTN_SCALE = HEAD_DIM ** -0.5
LOG2_E = 1.4426950408889634
RUNNING_MAX_FLOOR = -1e30
BF16_SUBLANES = 16
V_AUG_ROWS = HEAD_DIM + BF16_SUBLANES

LANES = 128
HEADS_PER_LANE_TILE = LANES // HEAD_DIM
TOKEN_TILE = 512
FRONT_TILE = 1024
SUB_TILE = 512
FFN_CHUNK = 256
VMEM_LIMIT_BYTES = 56 * 1024 * 1024

BF16 = jnp.bfloat16
F32 = jnp.float32

_NT = (((1,), (1,)), ((), ()))


def _layer_norm(y, g, b):
    mu = jnp.mean(y, axis=-1, keepdims=True)
    d = y - mu
    var = jnp.mean(d * d, axis=-1, keepdims=True)
    return d * lax.rsqrt(var + LN_EPS) * g + b


def _head_lane_mask(e):
    lane = lax.broadcasted_iota(jnp.int32, (1, LANES), 1)
    return (lane >= e * HEAD_DIM) & (lane < (e + 1) * HEAD_DIM)


def _mem_cross_attention(qm, kv_ref):
    qb = (qm * ATTN_SCALE).astype(BF16)
    v_all = kv_ref[:, W_MEM:]
    lane_v = lax.broadcasted_iota(jnp.int32, (1, W_MEM), 1)
    out = None
    for hm in range(N_MEM_HEADS):
        tile, e = divmod(hm, HEADS_PER_LANE_TILE)
        qp = qb[:, tile * LANES:(tile + 1) * LANES]
        qz = jnp.where(_head_lane_mask(e), qp, jnp.zeros_like(qp))
        kp = kv_ref[:, tile * LANES:(tile + 1) * LANES]
        s = lax.dot_general(qz, kp, _NT, preferred_element_type=F32)
        m = jnp.max(s, axis=-1, keepdims=True)
        p = jnp.exp(s - m)
        l = jnp.sum(p, axis=-1, keepdims=True)
        pn = (p * (1.0 / l)).astype(BF16)
        vz = jnp.where((lane_v >= hm * HEAD_DIM) & (lane_v < (hm + 1) * HEAD_DIM),
                       v_all, jnp.zeros_like(v_all))
        o = jnp.dot(pn, vz, preferred_element_type=F32)
        out = o if out is None else out + o
    return out


def _mem_kv_kernel(mem_ref, w_ref, o_ref):
    o_ref[0] = jnp.dot(mem_ref[...].astype(BF16), w_ref[0].astype(BF16),
                       preferred_element_type=F32).astype(BF16)


def _mem_kv(mem2d, w_mem_kv):
    rows, d = mem2d.shape
    depth, _, n = w_mem_kv.shape
    return pl.pallas_call(
        _mem_kv_kernel,
        grid=(depth,),
        in_specs=[pl.BlockSpec((rows, d), lambda i: (0, 0)),
                  pl.BlockSpec((1, d, n), lambda i: (i, 0, 0))],
        out_specs=pl.BlockSpec((1, rows, n), lambda i: (i, 0, 0)),
        out_shape=jax.ShapeDtypeStruct((depth, rows, n), BF16),
        compiler_params=pltpu.CompilerParams(dimension_semantics=("parallel",),
                                             vmem_limit_bytes=VMEM_LIMIT_BYTES),
        name="mem_kv",
    )(mem2d, w_mem_kv)


def _gmlp_front_kernel(x_ref, w_ref, lng_ref, lnb_ref, ws_ref, bs_ref, kv_ref,
                       mix_ref, mem_ref):
    row = lax.broadcasted_iota(jnp.int32, (GMLP_CHUNK, GMLP_CHUNK), 0)
    col = lax.broadcasted_iota(jnp.int32, (GMLP_CHUNK, GMLP_CHUNK), 1)
    causal = col <= row
    m0 = _head_lane_mask(0)
    m1 = _head_lane_mask(1)
    wcat = []
    for p in range(N_MIX_HEADS // HEADS_PER_LANE_TILE):
        w0 = ws_ref[2 * p]
        w1 = ws_ref[2 * p + 1]
        wcat.append(jnp.concatenate([jnp.where(causal, w0, jnp.zeros_like(w0)),
                                     jnp.where(causal, w1, jnp.zeros_like(w1))], axis=1))

    def sub_tile(r0):
        xb = x_ref[0, r0:r0 + SUB_TILE, :].astype(BF16)
        u = jax.nn.gelu(jnp.dot(xb, w_ref[:, :W_MIX], preferred_element_type=F32))
        gv = jax.nn.gelu(jnp.dot(xb, w_ref[:, W_MIX:2 * W_MIX], preferred_element_type=F32))
        vb = _layer_norm(gv, lng_ref[...], lnb_ref[...]).astype(BF16)
        qm = jnp.dot(xb, w_ref[:, 2 * W_MIX:], preferred_element_type=F32)
        for p in range(N_MIX_HEADS // HEADS_PER_LANE_TILE):
            lanes = slice(p * LANES, (p + 1) * LANES)
            bias = bs_ref[:, lanes]
            for c in range(SUB_TILE // GMLP_CHUNK):
                rows = slice(c * GMLP_CHUNK, (c + 1) * GMLP_CHUNK)
                vp = vb[rows, lanes]
                zero = jnp.zeros_like(vp)
                vbd = jnp.concatenate([jnp.where(m0, vp, zero), jnp.where(m1, vp, zero)], axis=0)
                sv = jnp.dot(wcat[p], vbd, preferred_element_type=F32) + bias
                mix_ref[0, r0 + c * GMLP_CHUNK:r0 + (c + 1) * GMLP_CHUNK, lanes] = (
                    u[rows, lanes] * sv).astype(BF16)
        mem_ref[0, r0:r0 + SUB_TILE, :] = _mem_cross_attention(qm, kv_ref.at[0]).astype(BF16)

    for r0 in range(0, x_ref.shape[1], SUB_TILE):
        sub_tile(r0)


def _gmlp_front(x, w_in, ln_g, ln_b, w_s, b_full, kv):
    b, s, d = x.shape
    t = FRONT_TILE
    n_in = w_in.shape[1]
    return pl.pallas_call(
        _gmlp_front_kernel,
        grid=(b, s // t),
        in_specs=[
            pl.BlockSpec((1, t, d), lambda bi, i: (bi, i, 0)),
            pl.BlockSpec((d, n_in), lambda bi, i: (0, 0)),
            pl.BlockSpec((1, W_MIX), lambda bi, i: (0, 0)),
            pl.BlockSpec((1, W_MIX), lambda bi, i: (0, 0)),
            pl.BlockSpec((N_MIX_HEADS, GMLP_CHUNK, GMLP_CHUNK), lambda bi, i: (0, 0, 0)),
            pl.BlockSpec((GMLP_CHUNK, W_MIX), lambda bi, i: (0, 0)),
            pl.BlockSpec((1, kv.shape[1], kv.shape[2]), lambda bi, i: (bi, 0, 0)),
        ],
        out_specs=[pl.BlockSpec((1, t, W_MIX), lambda bi, i: (bi, i, 0)),
                   pl.BlockSpec((1, t, W_MEM), lambda bi, i: (bi, i, 0))],
        out_shape=[jax.ShapeDtypeStruct((b, s, W_MIX), BF16),
                   jax.ShapeDtypeStruct((b, s, W_MEM), BF16)],
        compiler_params=pltpu.CompilerParams(dimension_semantics=("parallel", "parallel"),
                                             vmem_limit_bytes=VMEM_LIMIT_BYTES),
        name="gmlp_front",
    )(x, w_in, ln_g, ln_b, w_s, b_full, kv)


def _moba_front_kernel(x_ref, wqkm_ref, wvt_ref, kv_ref,
                       q_ref, k_ref, vt_ref, km_ref, mem_ref):
    ones = jnp.ones((V_AUG_ROWS - HEAD_DIM, MOBA_BLOCK), BF16)

    def sub_tile(r0):
        rows = slice(r0, r0 + SUB_TILE)
        xb = x_ref[0, rows, :].astype(BF16)
        q = jnp.dot(xb, wqkm_ref[:, :W_MIX], preferred_element_type=F32)
        q_ref[0, rows, :] = (q * (ATTN_SCALE * LOG2_E)).astype(BF16)
        k = jnp.dot(xb, wqkm_ref[:, W_MIX:2 * W_MIX], preferred_element_type=F32)
        k_ref[0, rows, :] = k.astype(BF16)
        vt = lax.dot_general(wvt_ref[...], xb, _NT, preferred_element_type=F32)
        for c in range(SUB_TILE // MOBA_BLOCK):
            blk = r0 // MOBA_BLOCK + c
            cols = slice(c * MOBA_BLOCK, (c + 1) * MOBA_BLOCK)
            km_ref[0, 0, blk:blk + 1, :] = jnp.mean(k[cols], axis=0, keepdims=True)
            for h in range(N_MIX_HEADS):
                vt_ref[0, blk, h, :HEAD_DIM, :] = vt[h * HEAD_DIM:(h + 1) * HEAD_DIM,
                                                     cols].astype(BF16)
                vt_ref[0, blk, h, HEAD_DIM:, :] = ones
        qm = jnp.dot(xb, wqkm_ref[:, 2 * W_MIX:], preferred_element_type=F32)
        mem_ref[0, rows, :] = _mem_cross_attention(qm, kv_ref.at[0]).astype(BF16)

    for r0 in range(0, x_ref.shape[1], SUB_TILE):
        sub_tile(r0)


def _moba_front(x, w_qkm, w_vt, kv):
    b, s, d = x.shape
    t = FRONT_TILE
    bpt = t // MOBA_BLOCK
    n_blk = s // MOBA_BLOCK
    return pl.pallas_call(
        _moba_front_kernel,
        grid=(b, s // t),
        in_specs=[
            pl.BlockSpec((1, t, d), lambda bi, i: (bi, i, 0)),
            pl.BlockSpec(w_qkm.shape, lambda bi, i: (0, 0)),
            pl.BlockSpec(w_vt.shape, lambda bi, i: (0, 0)),
            pl.BlockSpec((1, kv.shape[1], kv.shape[2]), lambda bi, i: (bi, 0, 0)),
        ],
        out_specs=[
            pl.BlockSpec((1, t, W_MIX), lambda bi, i: (bi, i, 0)),
            pl.BlockSpec((1, t, W_MIX), lambda bi, i: (bi, i, 0)),
            pl.BlockSpec((1, bpt, N_MIX_HEADS, V_AUG_ROWS, MOBA_BLOCK),
                         lambda bi, i: (bi, i, 0, 0, 0)),
            pl.BlockSpec((1, 1, bpt, W_MIX), lambda bi, i: (bi, i, 0, 0)),
            pl.BlockSpec((1, t, W_MEM), lambda bi, i: (bi, i, 0)),
        ],
        out_shape=[
            jax.ShapeDtypeStruct((b, s, W_MIX), BF16),
            jax.ShapeDtypeStruct((b, s, W_MIX), BF16),
            jax.ShapeDtypeStruct((b, n_blk, N_MIX_HEADS, V_AUG_ROWS, MOBA_BLOCK), BF16),
            jax.ShapeDtypeStruct((b, s // t, bpt, W_MIX), F32),
            jax.ShapeDtypeStruct((b, s, W_MEM), BF16),
        ],
        compiler_params=pltpu.CompilerParams(dimension_semantics=("parallel", "parallel"),
                                             vmem_limit_bytes=VMEM_LIMIT_BYTES),
        name="moba_front",
    )(x, w_qkm, w_vt, kv)


def _select_blocks(bs_t, n_past):
    n_blk = bs_t.shape[0]
    nidx = lax.broadcasted_iota(jnp.int32, bs_t.shape, 0)
    rem = nidx < n_past
    sel = jnp.zeros(bs_t.shape, jnp.bool_)
    for _ in range(MOBA_TOPK):
        ms = jnp.where(rem, bs_t, -jnp.inf)
        mx = jnp.max(ms, axis=0, keepdims=True)
        is_max = rem & (ms == mx)
        first = jnp.min(jnp.where(is_max, nidx, n_blk), axis=0, keepdims=True)
        pick = is_max & (nidx == first)
        sel = sel | pick
        rem = rem & jnp.logical_not(pick)
    return sel


def _moba_work_items(n_blk):
    items = []
    for j in range(n_blk):
        own = j // 2
        items.append(j | own << 8 | (j % 2 == 0) << 16 | (j % 2 == 1) << 17 | 1 << 18)
        items.extend(j | i << 8 for i in range(own))
    items.append(items[-1])
    return items


def _moba_attn_kernel(tab_ref, q_ref, k_ref, vt_ref, km_ref, o_ref,
                      qz_ref, bias_ref, cz_ref, s_ref, p_ref, sums_ref):
    nq = MOBA_BLOCK
    heads = range(HEADS_PER_LANE_TILE)
    width = HEADS_PER_LANE_TILE * nq
    n_blk = bias_ref.shape[0]
    n_items = tab_ref.shape[0] - 1

    def head_cols(a, e):
        return a[:, e * nq:(e + 1) * nq]

    def query_rows(tile):
        return pl.ds(pl.multiple_of(tile * nq, nq), nq)

    krow = lax.broadcasted_iota(jnp.int32, (nq, width), 0)
    qcol = lax.broadcasted_iota(jnp.int32, (nq, width), 1) & (nq - 1)
    cz_ref[0] = jnp.zeros((nq, width), F32)
    cz_ref[1] = jnp.where(krow <= qcol, 0.0, -jnp.inf).astype(F32)

    kmb = km_ref[0].astype(BF16)
    nidx = lax.broadcasted_iota(jnp.int32, (n_blk, width), 0)

    def select(j, carry):
        qp = q_ref[0, query_rows(j), :]
        zero = jnp.zeros_like(qp)
        qz = jnp.concatenate([jnp.where(_head_lane_mask(e), qp, zero) for e in heads], axis=0)
        qz_ref[j] = qz.T
        bs_t = lax.dot_general(kmb, qz, _NT, preferred_element_type=F32)
        keep = _select_blocks(bs_t, j) | (nidx == j)
        bias_ref[j] = jnp.where(keep, 0.0, -jnp.inf).astype(F32)
        return carry

    lax.fori_loop(0, n_blk, select, 0, unroll=4)

    def decode(w):
        code = tab_ref[w]
        return code & 0xFF, (code >> 8) & 0xFF, (code >> 16) & 1, (code >> 17) & 1, (code >> 18) & 1

    def block_bias(w):
        tile, pair = decode(w)[:2]
        return bias_ref[tile, pl.ds(2 * pair, 1), :], bias_ref[tile, pl.ds(2 * pair + 1, 1), :]

    def masked_scores(w, s_buf):
        tile, pair, causal_lo, causal_hi, _ = decode(w)
        s = jnp.dot(k_ref[0, pair], qz_ref[tile], preferred_element_type=F32)
        s0 = s[:nq] + cz_ref[causal_lo]
        s1 = s[nq:] + cz_ref[causal_hi]
        s_buf[:nq] = s0
        s_buf[nq:] = s1
        b0, b1 = block_bias(w)
        return jnp.maximum(jnp.max(s0, axis=0, keepdims=True) + b0,
                           jnp.max(s1, axis=0, keepdims=True) + b1)

    def probabilities(w, m, cmax, s_buf):
        first = decode(w)[4] == 1
        m = jnp.where(first, jnp.full_like(m, RUNNING_MAX_FLOOR), m)
        m_new = jnp.maximum(m, cmax)
        b0, b1 = block_bias(w)
        p0 = jnp.exp2(s_buf[:nq] - (m_new - b0)).astype(BF16)
        p1 = jnp.exp2(s_buf[nq:] - (m_new - b1)).astype(BF16)
        for e in heads:
            p_ref[e, :nq] = head_cols(p0, e)
            p_ref[e, nq:] = head_cols(p1, e)
        return m_new, jnp.where(first, jnp.zeros_like(m), jnp.exp2(m - m_new))

    def accumulate(w, alpha, acc):
        tile, pair = decode(w)[:2]
        acc = tuple(
            head_cols(alpha, e) * acc[e]
            + (jnp.dot(vt_ref[0, 2 * pair, e], p_ref[e, :nq], preferred_element_type=F32)
               + jnp.dot(vt_ref[0, 2 * pair + 1, e], p_ref[e, nq:], preferred_element_type=F32))
            for e in heads)
        for e in heads:
            sums_ref[tile, e] = acc[e]
        return acc

    def one_item(w, carry):
        m, acc, cmax, alpha = carry
        acc = accumulate(jnp.maximum(w - 1, 0), alpha, acc)
        m, alpha = probabilities(w, m, cmax, s_ref)
        return m, acc, masked_scores(w + 1, s_ref), alpha

    p_ref[...] = jnp.zeros(p_ref.shape, BF16)
    carry = (jnp.zeros((1, width), F32), tuple(jnp.ones((V_AUG_ROWS, nq), F32) for _ in heads),
             masked_scores(0, s_ref), jnp.ones((1, width), F32))
    m, acc, _, alpha = lax.fori_loop(0, n_items, one_item, carry, unroll=2)
    accumulate(n_items - 1, alpha, acc)

    def normalise(j, carry):
        o_t = jnp.concatenate(
            [sums_ref[j, e, :HEAD_DIM] / sums_ref[j, e, HEAD_DIM:HEAD_DIM + 1] for e in heads],
            axis=0)
        o_ref[0, query_rows(j), :] = o_t.T.astype(BF16)
        return carry

    lax.fori_loop(0, n_blk, normalise, 0, unroll=4)


def _moba_attn(q, k_pairs, vt, km):
    b, s, _ = q.shape
    n_blk = s // MOBA_BLOCK
    n_tiles = W_MIX // LANES
    width = HEADS_PER_LANE_TILE * MOBA_BLOCK
    table = jnp.asarray(_moba_work_items(n_blk), jnp.int32)
    grid_spec = pltpu.PrefetchScalarGridSpec(
        num_scalar_prefetch=1,
        grid=(b, n_tiles),
        in_specs=[
            pl.BlockSpec((1, s, LANES), lambda bi, p, tab: (bi, 0, p)),
            pl.BlockSpec((1, n_blk // 2, 2 * MOBA_BLOCK, LANES), lambda bi, p, tab: (bi, 0, 0, p)),
            pl.BlockSpec((1, n_blk, HEADS_PER_LANE_TILE, V_AUG_ROWS, MOBA_BLOCK),
                         lambda bi, p, tab: (bi, 0, p, 0, 0)),
            pl.BlockSpec((1, n_blk, LANES), lambda bi, p, tab: (bi, 0, p)),
        ],
        out_specs=pl.BlockSpec((1, s, LANES), lambda bi, p, tab: (bi, 0, p)),
        scratch_shapes=[
            pltpu.VMEM((n_blk, LANES, width), BF16),
            pltpu.VMEM((n_blk, n_blk, width), F32),
            pltpu.VMEM((2, MOBA_BLOCK, width), F32),
            pltpu.VMEM((2 * MOBA_BLOCK, width), F32),
            pltpu.VMEM((HEADS_PER_LANE_TILE, 2 * MOBA_BLOCK, MOBA_BLOCK), BF16),
            pltpu.VMEM((n_blk, HEADS_PER_LANE_TILE, V_AUG_ROWS, MOBA_BLOCK), F32),
        ],
    )
    return pl.pallas_call(
        _moba_attn_kernel,
        grid_spec=grid_spec,
        out_shape=jax.ShapeDtypeStruct((b, s, W_MIX), BF16),
        compiler_params=pltpu.CompilerParams(
            dimension_semantics=("parallel", "parallel"),
            vmem_limit_bytes=VMEM_LIMIT_BYTES),
        name="moba_attn",
    )(table, q, k_pairs, vt, km)


def _post_mixer_kernel(x_ref, mix_ref, mem_ref, wo_ref, g1_ref, b1_ref,
                       wi_ref, wd_ref, g2_ref, b2_ref, o_ref, act_ref):
    sub = (jnp.dot(mix_ref[0], wo_ref[0, :W_MIX, :], preferred_element_type=F32)
           + jnp.dot(mem_ref[0], wo_ref[0, W_MIX:, :], preferred_element_type=F32))
    x1 = _layer_norm(ALPHA * x_ref[0] + sub, g1_ref[0], b1_ref[0])
    x1b = x1.astype(BF16)
    d_ff = wd_ref.shape[1]
    for c in range(d_ff // FFN_CHUNK):
        cols = slice(c * FFN_CHUNK, (c + 1) * FFN_CHUNK)
        up_cols = slice(d_ff + c * FFN_CHUNK, d_ff + (c + 1) * FFN_CHUNK)
        g = jnp.dot(x1b, wi_ref[0, :, cols], preferred_element_type=F32)
        u = jnp.dot(x1b, wi_ref[0, :, up_cols], preferred_element_type=F32)
        act_ref[:, cols] = (jax.nn.silu(g) * u).astype(BF16)
    y = jnp.dot(act_ref[...], wd_ref[0], preferred_element_type=F32)
    o_ref[0] = _layer_norm(ALPHA * x1 + y, g2_ref[0], b2_ref[0])


def _post_mixer(layer, x, mix, mem_out, wo, g1, b1, wi, wd, g2, b2):
    b, s, d = x.shape
    t = TOKEN_TILE
    d_ff = wd.shape[1]
    single = pl.Buffered(1)

    def wspec(arr):
        return pl.BlockSpec((1,) + arr.shape[1:], lambda bi, i: (layer,) + (0,) * (arr.ndim - 1),
                            pipeline_mode=single)

    return pl.pallas_call(
        _post_mixer_kernel,
        grid=(b, s // t),
        in_specs=[
            pl.BlockSpec((1, t, d), lambda bi, i: (bi, i, 0)),
            pl.BlockSpec((1, t, W_MIX), lambda bi, i: (bi, i, 0)),
            pl.BlockSpec((1, t, W_MEM), lambda bi, i: (bi, i, 0)),
            wspec(wo), wspec(g1), wspec(b1), wspec(wi), wspec(wd), wspec(g2), wspec(b2),
        ],
        out_specs=pl.BlockSpec((1, t, d), lambda bi, i: (bi, i, 0)),
        out_shape=jax.ShapeDtypeStruct((b, s, d), F32),
        scratch_shapes=[pltpu.VMEM((t, d_ff), BF16)],
        compiler_params=pltpu.CompilerParams(dimension_semantics=("parallel", "parallel"),
                                             vmem_limit_bytes=VMEM_LIMIT_BYTES),
        name="post_mixer",
    )(x, mix, mem_out, wo, g1, b1, wi, wd, g2, b2)


def kernel(x, mem, a_w_in, a_ln_v_g, a_ln_v_b, a_w_s, a_b_s, b_w_in, w_mem_kv, w_mix_out,
           ln_mix_g, ln_mix_b, w_ffn_in, w_ffn_out, ln_ffn_g, ln_ffn_b):
    b, s, d = x.shape
    n_mem = mem.shape[1]
    d_ff = w_ffn_out.shape[1]
    assert s % FRONT_TILE == 0 and FRONT_TILE % SUB_TILE == 0 and SUB_TILE % MOBA_BLOCK == 0
    assert s % TOKEN_TILE == 0 and d_ff % FFN_CHUNK == 0
    n_blk = s // MOBA_BLOCK
    assert n_blk % 2 == 0

    kv = _mem_kv(mem.reshape(b * n_mem, d), w_mem_kv).reshape(DEPTH, b, n_mem, 2 * W_MEM)

    wo, wi, wd = (w.astype(BF16) for w in (w_mix_out, w_ffn_in, w_ffn_out))
    ln_rows = [v.reshape(DEPTH, 1, d) for v in (ln_mix_g, ln_mix_b, ln_ffn_g, ln_ffn_b)]

    def post(i, xin, mix, mem_out):
        return _post_mixer(i, xin, mix, mem_out, wo, ln_rows[0], ln_rows[1], wi, wd,
                           ln_rows[2], ln_rows[3])

    b_full = jnp.repeat(a_b_s[0].T, HEAD_DIM, axis=1)
    mix, mem_out = _gmlp_front(x, a_w_in[0].astype(BF16), a_ln_v_g[0].reshape(1, -1),
                               a_ln_v_b[0].reshape(1, -1), a_w_s[0].astype(BF16), b_full, kv[0])
    x = post(0, x, mix, mem_out)

    w1 = b_w_in[0].astype(BF16)
    w_qkm = jnp.concatenate([w1[:, :2 * W_MIX], w1[:, 3 * W_MIX:]], axis=1)
    w_vt = w1[:, 2 * W_MIX:3 * W_MIX].T
    q, k, vt, km, mem_out = _moba_front(x, w_qkm, w_vt, kv[1])
    mix = _moba_attn(q, k.reshape(b, n_blk // 2, 2 * MOBA_BLOCK, W_MIX), vt,
                     km.reshape(b, n_blk, W_MIX))
    return post(1, x, mix, mem_out)
```

```python
import functools

import jax
import jax.numpy as jnp
from jax import lax
from jax.experimental import pallas as pl
from jax.experimental.pallas import tpu as pltpu

HEAD_DIM = 64
N_MIX_HEADS = 12
W_MIX = N_MIX_HEADS * HEAD_DIM
N_MEM_HEADS = 4
W_MEM = N_MEM_HEADS * HEAD_DIM
GMLP_CHUNK = 128
MOBA_BLOCK = 256
MOBA_TOPK = 3
DEPTH = 2
ALPHA = (2.0 * DEPTH) ** 0.25
LN_EPS = 1e-5
ATTN_SCALE = HEAD_DIM ** -0.5
LOG2_E = 1.4426950408889634
RUNNING_MAX_FLOOR = -1e30
BF16_SUBLANES = 16
V_AUG_ROWS = HEAD_DIM + BF16_SUBLANES

LANES = 128
HEADS_PER_LANE_TILE = LANES // HEAD_DIM
TOKEN_TILE = 512
FRONT_TILE = 1024
SUB_TILE = 512
FFN_CHUNK = 256
SELECT_GROUP = 8
VMEM_LIMIT_BYTES = 56 * 1024 * 1024

BF16 = jnp.bfloat16
F32 = jnp.float32

_NT = (((1,), (1,)), ((), ()))


def _layer_norm(y, g, b):
    mu = jnp.mean(y, axis=-1, keepdims=True)
    d = y - mu
    var = jnp.mean(d * d, axis=-1, keepdims=True)
    return d * lax.rsqrt(var + LN_EPS) * g + b


def _head_lane_mask(e):
    lane = lax.broadcasted_iota(jnp.int32, (1, LANES), 1)
    return (lane >= e * HEAD_DIM) & (lane < (e + 1) * HEAD_DIM)


def _mem_cross_attention(qm, kv_ref):
    qb = (qm * ATTN_SCALE).astype(BF16)
    v_all = kv_ref[:, W_MEM:]
    lane_v = lax.broadcasted_iota(jnp.int32, (1, W_MEM), 1)
    out = None
    for hm in range(N_MEM_HEADS):
        tile, e = divmod(hm, HEADS_PER_LANE_TILE)
        qp = qb[:, tile * LANES:(tile + 1) * LANES]
        qz = jnp.where(_head_lane_mask(e), qp, jnp.zeros_like(qp))
        kp = kv_ref[:, tile * LANES:(tile + 1) * LANES]
        s = lax.dot_general(qz, kp, _NT, preferred_element_type=F32)
        m = jnp.max(s, axis=-1, keepdims=True)
        p = jnp.exp(s - m)
        l = jnp.sum(p, axis=-1, keepdims=True)
        pn = (p * (1.0 / l)).astype(BF16)
        vz = jnp.where((lane_v >= hm * HEAD_DIM) & (lane_v < (hm + 1) * HEAD_DIM),
                       v_all, jnp.zeros_like(v_all))
        o = jnp.dot(pn, vz, preferred_element_type=F32)
        out = o if out is None else out + o
    return out


def _mem_kv_kernel(mem_ref, w_ref, o_ref):
    o_ref[0] = jnp.dot(mem_ref[...].astype(BF16), w_ref[0].astype(BF16),
                       preferred_element_type=F32).astype(BF16)


def _mem_kv(mem2d, w_mem_kv):
    rows, d = mem2d.shape
    depth, _, n = w_mem_kv.shape
    return pl.pallas_call(
        _mem_kv_kernel,
        grid=(depth,),
        in_specs=[pl.BlockSpec((rows, d), lambda i: (0, 0)),
                  pl.BlockSpec((1, d, n), lambda i: (i, 0, 0))],
        out_specs=pl.BlockSpec((1, rows, n), lambda i: (i, 0, 0)),
        out_shape=jax.ShapeDtypeStruct((depth, rows, n), BF16),
        compiler_params=pltpu.CompilerParams(dimension_semantics=("parallel",),
                                             vmem_limit_bytes=VMEM_LIMIT_BYTES),
        name="mem_kv",
    )(mem2d, w_mem_kv)


def _gmlp_front_kernel(x_ref, w_ref, lng_ref, lnb_ref, ws_ref, bs_ref, kv_ref,
                       mix_ref, mem_ref):
    row = lax.broadcasted_iota(jnp.int32, (GMLP_CHUNK, GMLP_CHUNK), 0)
    col = lax.broadcasted_iota(jnp.int32, (GMLP_CHUNK, GMLP_CHUNK), 1)
    causal = col <= row
    m0 = _head_lane_mask(0)
    m1 = _head_lane_mask(1)
    wcat = []
    for p in range(N_MIX_HEADS // HEADS_PER_LANE_TILE):
        w0 = ws_ref[2 * p]
        w1 = ws_ref[2 * p + 1]
        wcat.append(jnp.concatenate([jnp.where(causal, w0, jnp.zeros_like(w0)),
                                     jnp.where(causal, w1, jnp.zeros_like(w1))], axis=1))

    def sub_tile(r0):
        xb = x_ref[0, r0:r0 + SUB_TILE, :].astype(BF16)
        u = jax.nn.gelu(jnp.dot(xb, w_ref[:, :W_MIX], preferred_element_type=F32))
        gv = jax.nn.gelu(jnp.dot(xb, w_ref[:, W_MIX:2 * W_MIX], preferred_element_type=F32))
        vb = _layer_norm(gv, lng_ref[...], lnb_ref[...]).astype(BF16)
        qm = jnp.dot(xb, w_ref[:, 2 * W_MIX:], preferred_element_type=F32)
        for p in range(N_MIX_HEADS // HEADS_PER_LANE_TILE):
            lanes = slice(p * LANES, (p + 1) * LANES)
            bias = bs_ref[:, lanes]
            for c in range(SUB_TILE // GMLP_CHUNK):
                rows = slice(c * GMLP_CHUNK, (c + 1) * GMLP_CHUNK)
                vp = vb[rows, lanes]
                zero = jnp.zeros_like(vp)
                vbd = jnp.concatenate([jnp.where(m0, vp, zero), jnp.where(m1, vp, zero)], axis=0)
                sv = jnp.dot(wcat[p], vbd, preferred_element_type=F32) + bias
                mix_ref[0, r0 + c * GMLP_CHUNK:r0 + (c + 1) * GMLP_CHUNK, lanes] = (
                    u[rows, lanes] * sv).astype(BF16)
        mem_ref[0, r0:r0 + SUB_TILE, :] = _mem_cross_attention(qm, kv_ref.at[0]).astype(BF16)

    for r0 in range(0, x_ref.shape[1], SUB_TILE):
        sub_tile(r0)


def _gmlp_front(x, w_in, ln_g, ln_b, w_s, b_full, kv):
    b, s, d = x.shape
    t = FRONT_TILE
    n_in = w_in.shape[1]
    return pl.pallas_call(
        _gmlp_front_kernel,
        grid=(b, s // t),
        in_specs=[
            pl.BlockSpec((1, t, d), lambda bi, i: (bi, i, 0)),
            pl.BlockSpec((d, n_in), lambda bi, i: (0, 0)),
            pl.BlockSpec((1, W_MIX), lambda bi, i: (0, 0)),
            pl.BlockSpec((1, W_MIX), lambda bi, i: (0, 0)),
            pl.BlockSpec((N_MIX_HEADS, GMLP_CHUNK, GMLP_CHUNK), lambda bi, i: (0, 0, 0)),
            pl.BlockSpec((GMLP_CHUNK, W_MIX), lambda bi, i: (0, 0)),
            pl.BlockSpec((1, kv.shape[1], kv.shape[2]), lambda bi, i: (bi, 0, 0)),
        ],
        out_specs=[pl.BlockSpec((1, t, W_MIX), lambda bi, i: (bi, i, 0)),
                   pl.BlockSpec((1, t, W_MEM), lambda bi, i: (bi, i, 0))],
        out_shape=[jax.ShapeDtypeStruct((b, s, W_MIX), BF16),
                   jax.ShapeDtypeStruct((b, s, W_MEM), BF16)],
        compiler_params=pltpu.CompilerParams(dimension_semantics=("parallel", "parallel"),
                                             vmem_limit_bytes=VMEM_LIMIT_BYTES),
        name="gmlp_front",
    )(x, w_in, ln_g, ln_b, w_s, b_full, kv)


def _moba_front_kernel(x_ref, wqkm_ref, wvt_ref, kv_ref,
                       q_ref, k_ref, vt_ref, km_ref, mem_ref):
    ones = jnp.ones((V_AUG_ROWS - HEAD_DIM, MOBA_BLOCK), BF16)

    def sub_tile(r0):
        rows = slice(r0, r0 + SUB_TILE)
        xb = x_ref[0, rows, :].astype(BF16)
        q = jnp.dot(xb, wqkm_ref[:, :W_MIX], preferred_element_type=F32)
        q_ref[0, rows, :] = (q * (ATTN_SCALE * LOG2_E)).astype(BF16)
        k = jnp.dot(xb, wqkm_ref[:, W_MIX:2 * W_MIX], preferred_element_type=F32)
        k_ref[0, rows, :] = k.astype(BF16)
        vt = lax.dot_general(wvt_ref[...], xb, _NT, preferred_element_type=F32)
        for c in range(SUB_TILE // MOBA_BLOCK):
            blk = r0 // MOBA_BLOCK + c
            cols = slice(c * MOBA_BLOCK, (c + 1) * MOBA_BLOCK)
            km_ref[0, 0, blk:blk + 1, :] = jnp.mean(k[cols], axis=0, keepdims=True)
            for h in range(N_MIX_HEADS):
                vt_ref[0, blk, h, :HEAD_DIM, :] = vt[h * HEAD_DIM:(h + 1) * HEAD_DIM,
                                                     cols].astype(BF16)
                vt_ref[0, blk, h, HEAD_DIM:, :] = ones
        qm = jnp.dot(xb, wqkm_ref[:, 2 * W_MIX:], preferred_element_type=F32)
        mem_ref[0, rows, :] = _mem_cross_attention(qm, kv_ref.at[0]).astype(BF16)

    for r0 in range(0, x_ref.shape[1], SUB_TILE):
        sub_tile(r0)


def _moba_front(x, w_qkm, w_vt, kv):
    b, s, d = x.shape
    t = FRONT_TILE
    bpt = t // MOBA_BLOCK
    n_blk = s // MOBA_BLOCK
    return pl.pallas_call(
        _moba_front_kernel,
        grid=(b, s // t),
        in_specs=[
            pl.BlockSpec((1, t, d), lambda bi, i: (bi, i, 0)),
            pl.BlockSpec(w_qkm.shape, lambda bi, i: (0, 0)),
            pl.BlockSpec(w_vt.shape, lambda bi, i: (0, 0)),
            pl.BlockSpec((1, kv.shape[1], kv.shape[2]), lambda bi, i: (bi, 0, 0)),
        ],
        out_specs=[
            pl.BlockSpec((1, t, W_MIX), lambda bi, i: (bi, i, 0)),
            pl.BlockSpec((1, t, W_MIX), lambda bi, i: (bi, i, 0)),
            pl.BlockSpec((1, bpt, N_MIX_HEADS, V_AUG_ROWS, MOBA_BLOCK),
                         lambda bi, i: (bi, i, 0, 0, 0)),
            pl.BlockSpec((1, 1, bpt, W_MIX), lambda bi, i: (bi, i, 0, 0)),
            pl.BlockSpec((1, t, W_MEM), lambda bi, i: (bi, i, 0)),
        ],
        out_shape=[
            jax.ShapeDtypeStruct((b, s, W_MIX), BF16),
            jax.ShapeDtypeStruct((b, s, W_MIX), BF16),
            jax.ShapeDtypeStruct((b, n_blk, N_MIX_HEADS, V_AUG_ROWS, MOBA_BLOCK), BF16),
            jax.ShapeDtypeStruct((b, s // t, bpt, W_MIX), F32),
            jax.ShapeDtypeStruct((b, s, W_MEM), BF16),
        ],
        compiler_params=pltpu.CompilerParams(dimension_semantics=("parallel", "parallel"),
                                             vmem_limit_bytes=VMEM_LIMIT_BYTES),
        name="moba_front",
    )(x, w_qkm, w_vt, kv)


def _select_blocks(bs_t, n_past):
    n_blk = bs_t.shape[0]
    nidx = lax.broadcasted_iota(jnp.int32, bs_t.shape, 0)
    rem = nidx < n_past
    sel = jnp.zeros(bs_t.shape, jnp.bool_)
    for _ in range(MOBA_TOPK):
        ms = jnp.where(rem, bs_t, -jnp.inf)
        mx = jnp.max(ms, axis=0, keepdims=True)
        is_max = rem & (ms == mx)
        first = jnp.min(jnp.where(is_max, nidx, n_blk), axis=0, keepdims=True)
        pick = is_max & (nidx == first)
        sel = sel | pick
        rem = rem & jnp.logical_not(pick)
    return sel


def _moba_work_items(n_blk):
    items = []
    for j in range(n_blk):
        own = j // 2
        items.append(j | own << 8 | (j % 2 == 0) << 16 | (j % 2 == 1) << 17 | 1 << 18)
        items.extend(j | i << 8 for i in range(own))
    items.append(items[-1])
    return items


def _moba_attn_kernel(tab_ref, q_ref, k_ref, vt_ref, km_ref, o_ref,
                      qz_ref, bias_ref, cz_ref, s_ref, p_ref, sums_ref):
    nq = MOBA_BLOCK
    heads = range(HEADS_PER_LANE_TILE)
    width = HEADS_PER_LANE_TILE * nq
    n_blk = bias_ref.shape[0]
    n_items = tab_ref.shape[0] - 1

    def head_cols(a, e):
        return a[:, e * nq:(e + 1) * nq]

    def query_rows(tile):
        return pl.ds(pl.multiple_of(tile * nq, nq), nq)

    krow = lax.broadcasted_iota(jnp.int32, (nq, width), 0)
    qcol = lax.broadcasted_iota(jnp.int32, (nq, width), 1) & (nq - 1)
    cz_ref[0] = jnp.zeros((nq, width), F32)
    cz_ref[1] = jnp.where(krow <= qcol, 0.0, -jnp.inf).astype(F32)

    kmb = km_ref[0].astype(BF16)

    def select(rows, j, carry):
        qp = q_ref[0, query_rows(j), :]
        zero = jnp.zeros_like(qp)
        qz = jnp.concatenate([jnp.where(_head_lane_mask(e), qp, zero) for e in heads], axis=0)
        qz_t = qz.T
        qz_ref[j] = qz_t
        bs_t = jnp.dot(kmb[:rows], qz_t, preferred_element_type=F32)
        nidx = lax.broadcasted_iota(jnp.int32, (rows, width), 0)
        keep = _select_blocks(bs_t, j) | (nidx == j)
        bias_ref[j, :rows] = jnp.where(keep, 0.0, -jnp.inf).astype(F32)
        if rows < n_blk:
            bias_ref[j, rows:] = jnp.full((n_blk - rows, width), -jnp.inf, F32)
        return carry

    for j0 in range(0, n_blk, SELECT_GROUP):
        lax.fori_loop(j0, j0 + SELECT_GROUP, functools.partial(select, j0 + SELECT_GROUP), 0,
                      unroll=4)

    def decode(w):
        code = tab_ref[w]
        return code & 0xFF, (code >> 8) & 0xFF, (code >> 16) & 1, (code >> 17) & 1, (code >> 18) & 1

    def block_bias(w):
        tile, pair = decode(w)[:2]
        return bias_ref[tile, pl.ds(2 * pair, 1), :], bias_ref[tile, pl.ds(2 * pair + 1, 1), :]

    def masked_scores(w, s_buf):
        tile, pair, causal_lo, causal_hi, _ = decode(w)
        s = jnp.dot(k_ref[0, pair], qz_ref[tile], preferred_element_type=F32)
        s0 = s[:nq] + cz_ref[causal_lo]
        s1 = s[nq:] + cz_ref[causal_hi]
        s_buf[:nq] = s0
        s_buf[nq:] = s1
        b0, b1 = block_bias(w)
        return jnp.maximum(jnp.max(s0, axis=0, keepdims=True) + b0,
                           jnp.max(s1, axis=0, keepdims=True) + b1)

    def probabilities(w, m, cmax, s_buf):
        first = decode(w)[4] == 1
        m = jnp.where(first, jnp.full_like(m, RUNNING_MAX_FLOOR), m)
        m_new = jnp.maximum(m, cmax)
        b0, b1 = block_bias(w)
        p0 = jnp.exp2(s_buf[:nq] - (m_new - b0)).astype(BF16)
        p1 = jnp.exp2(s_buf[nq:] - (m_new - b1)).astype(BF16)
        for e in heads:
            p_ref[e, :nq] = head_cols(p0, e)
            p_ref[e, nq:] = head_cols(p1, e)
        return m_new, jnp.where(first, jnp.zeros_like(m), jnp.exp2(m - m_new))

    def accumulate(w, alpha, acc):
        tile, pair = decode(w)[:2]
        acc = tuple(
            head_cols(alpha, e) * acc[e]
            + (jnp.dot(vt_ref[0, 2 * pair, e], p_ref[e, :nq], preferred_element_type=F32)
               + jnp.dot(vt_ref[0, 2 * pair + 1, e], p_ref[e, nq:], preferred_element_type=F32))
            for e in heads)
        for e in heads:
            sums_ref[tile, e] = acc[e]
        return acc

    def one_item(w, carry):
        m, acc, cmax, alpha = carry
        acc = accumulate(jnp.maximum(w - 1, 0), alpha, acc)
        m, alpha = probabilities(w, m, cmax, s_ref)
        return m, acc, masked_scores(w + 1, s_ref), alpha

    p_ref[...] = jnp.zeros(p_ref.shape, BF16)
    carry = (jnp.zeros((1, width), F32), tuple(jnp.ones((V_AUG_ROWS, nq), F32) for _ in heads),
             masked_scores(0, s_ref), jnp.ones((1, width), F32))
    m, acc, _, alpha = lax.fori_loop(0, n_items, one_item, carry, unroll=2)
    accumulate(n_items - 1, alpha, acc)

    def normalise(j, carry):
        o_t = jnp.concatenate(
            [sums_ref[j, e, :HEAD_DIM] / sums_ref[j, e, HEAD_DIM:HEAD_DIM + 1] for e in heads],
            axis=0)
        o_ref[0, query_rows(j), :] = o_t.T.astype(BF16)
        return carry

    lax.fori_loop(0, n_blk, normalise, 0, unroll=4)


def _moba_attn(q, k_pairs, vt, km):
    b, s, _ = q.shape
    n_blk = s // MOBA_BLOCK
    n_tiles = W_MIX // LANES
    width = HEADS_PER_LANE_TILE * MOBA_BLOCK
    table = jnp.asarray(_moba_work_items(n_blk), jnp.int32)
    grid_spec = pltpu.PrefetchScalarGridSpec(
        num_scalar_prefetch=1,
        grid=(b, n_tiles),
        in_specs=[
            pl.BlockSpec((1, s, LANES), lambda bi, p, tab: (bi, 0, p)),
            pl.BlockSpec((1, n_blk // 2, 2 * MOBA_BLOCK, LANES), lambda bi, p, tab: (bi, 0, 0, p)),
            pl.BlockSpec((1, n_blk, HEADS_PER_LANE_TILE, V_AUG_ROWS, MOBA_BLOCK),
                         lambda bi, p, tab: (bi, 0, p, 0, 0)),
            pl.BlockSpec((1, n_blk, LANES), lambda bi, p, tab: (bi, 0, p)),
        ],
        out_specs=pl.BlockSpec((1, s, LANES), lambda bi, p, tab: (bi, 0, p)),
        scratch_shapes=[
            pltpu.VMEM((n_blk, LANES, width), BF16),
            pltpu.VMEM((n_blk, n_blk, width), F32),
            pltpu.VMEM((2, MOBA_BLOCK, width), F32),
            pltpu.VMEM((2 * MOBA_BLOCK, width), F32),
            pltpu.VMEM((HEADS_PER_LANE_TILE, 2 * MOBA_BLOCK, MOBA_BLOCK), BF16),
            pltpu.VMEM((n_blk, HEADS_PER_LANE_TILE, V_AUG_ROWS, MOBA_BLOCK), F32),
        ],
    )
    return pl.pallas_call(
        _moba_attn_kernel,
        grid_spec=grid_spec,
        out_shape=jax.ShapeDtypeStruct((b, s, W_MIX), BF16),
        compiler_params=pltpu.CompilerParams(
            dimension_semantics=("parallel", "parallel"),
            vmem_limit_bytes=VMEM_LIMIT_BYTES),
        name="moba_attn",
    )(table, q, k_pairs, vt, km)


def _post_mixer_kernel(x_ref, mix_ref, mem_ref, wo_ref, g1_ref, b1_ref,
                       wi_ref, wd_ref, g2_ref, b2_ref, o_ref, act_ref):
    sub = (jnp.dot(mix_ref[0], wo_ref[0, :W_MIX, :], preferred_element_type=F32)
           + jnp.dot(mem_ref[0], wo_ref[0, W_MIX:, :], preferred_element_type=F32))
    x1 = _layer_norm(ALPHA * x_ref[0] + sub, g1_ref[0], b1_ref[0])
    x1b = x1.astype(BF16)
    d_ff = wd_ref.shape[1]
    for c in range(d_ff // FFN_CHUNK):
        cols = slice(c * FFN_CHUNK, (c + 1) * FFN_CHUNK)
        up_cols = slice(d_ff + c * FFN_CHUNK, d_ff + (c + 1) * FFN_CHUNK)
        g = jnp.dot(x1b, wi_ref[0, :, cols], preferred_element_type=F32)
        u = jnp.dot(x1b, wi_ref[0, :, up_cols], preferred_element_type=F32)
        act_ref[:, cols] = (jax.nn.silu(g) * u).astype(BF16)
    y = jnp.dot(act_ref[...], wd_ref[0], preferred_element_type=F32)
    o_ref[0] = _layer_norm(ALPHA * x1 + y, g2_ref[0], b2_ref[0])


def _post_mixer(layer, x, mix, mem_out, wo, g1, b1, wi, wd, g2, b2):
    b, s, d = x.shape
    t = TOKEN_TILE
    d_ff = wd.shape[1]
    single = pl.Buffered(1)

    def wspec(arr):
        return pl.BlockSpec((1,) + arr.shape[1:], lambda bi, i: (layer,) + (0,) * (arr.ndim - 1),
                            pipeline_mode=single)

    return pl.pallas_call(
        _post_mixer_kernel,
        grid=(b, s // t),
        in_specs=[
            pl.BlockSpec((1, t, d), lambda bi, i: (bi, i, 0)),
            pl.BlockSpec((1, t, W_MIX), lambda bi, i: (bi, i, 0)),
            pl.BlockSpec((1, t, W_MEM), lambda bi, i: (bi, i, 0)),
            wspec(wo), wspec(g1), wspec(b1), wspec(wi), wspec(wd), wspec(g2), wspec(b2),
        ],
        out_specs=pl.BlockSpec((1, t, d), lambda bi, i: (bi, i, 0)),
        out_shape=jax.ShapeDtypeStruct((b, s, d), F32),
        scratch_shapes=[pltpu.VMEM((t, d_ff), BF16)],
        compiler_params=pltpu.CompilerParams(dimension_semantics=("parallel", "parallel"),
                                             vmem_limit_bytes=VMEM_LIMIT_BYTES),
        name="post_mixer",
    )(x, mix, mem_out, wo, g1, b1, wi, wd, g2, b2)


def kernel(x, mem, a_w_in, a_ln_v_g, a_ln_v_b, a_w_s, a_b_s, b_w_in, w_mem_kv, w_mix_out,
           ln_mix_g, ln_mix_b, w_ffn_in, w_ffn_out, ln_ffn_g, ln_ffn_b):
    b, s, d = x.shape
    n_mem = mem.shape[1]
    d_ff = w_ffn_out.shape[1]
    assert s % FRONT_TILE == 0 and FRONT_TILE % SUB_TILE == 0 and SUB_TILE % MOBA_BLOCK == 0
    assert s % TOKEN_TILE == 0 and d_ff % FFN_CHUNK == 0
    n_blk = s // MOBA_BLOCK
    assert n_blk % 2 == 0 and n_blk % SELECT_GROUP == 0

    kv = _mem_kv(mem.reshape(b * n_mem, d), w_mem_kv).reshape(DEPTH, b, n_mem, 2 * W_MEM)

    wo, wi, wd = (w.astype(BF16) for w in (w_mix_out, w_ffn_in, w_ffn_out))
    ln_rows = [v.reshape(DEPTH, 1, d) for v in (ln_mix_g, ln_mix_b, ln_ffn_g, ln_ffn_b)]

    def post(i, xin, mix, mem_out):
        return _post_mixer(i, xin, mix, mem_out, wo, ln_rows[0], ln_rows[1], wi, wd,
                           ln_rows[2], ln_rows[3])

    b_full = jnp.repeat(a_b_s[0].T, HEAD_DIM, axis=1)
    mix, mem_out = _gmlp_front(x, a_w_in[0].astype(BF16), a_ln_v_g[0].reshape(1, -1),
                               a_ln_v_b[0].reshape(1, -1), a_w_s[0].astype(BF16), b_full, kv[0])
    x = post(0, x, mix, mem_out)

    w1 = b_w_in[0].astype(BF16)
    w_qkm = jnp.concatenate([w1[:, :2 * W_MIX], w1[:, 3 * W_MIX:]], axis=1)
    w_vt = w1[:, 2 * W_MIX:3 * W_MIX].T
    q, k, vt, km, mem_out = _moba_front(x, w_qkm, w_vt, kv[1])
    mix = _moba_attn(q, k.reshape(b, n_blk // 2, 2 * MOBA_BLOCK, W_MIX), vt,
                     km.reshape(b, n_blk, W_MIX))
    return post(1, x, mix, mem_out)
```

```python
import functools

import jax
import jax.numpy as jnp
from jax import lax
from jax.experimental import pallas as pl
from jax.experimental.pallas import tpu as pltpu

HEAD_DIM = 64
N_MIX_HEADS = 12
W_MIX = N_MIX_HEADS * HEAD_DIM
N_MEM_HEADS = 4
W_MEM = N_MEM_HEADS * HEAD_DIM
GMLP_CHUNK = 128
MOBA_BLOCK = 256
MOBA_TOPK = 3
DEPTH = 2
ALPHA = (2.0 * DEPTH) ** 0.25
LN_EPS = 1e-5
ATTN_SCALE = HEAD_DIM ** -0.5
LOG2_E = 1.4426950408889634
RUNNING_MAX_FLOOR = -1e30
BF16_SUBLANES = 16
V_AUG_ROWS = HEAD_DIM + BF16_SUBLANES

LANES = 128
HEADS_PER_LANE_TILE = LANES // HEAD_DIM
TOKEN_TILE = 1024
FRONT_TILE = 1024
SUB_TILE = 512
FFN_CHUNK = 256
SELECT_GROUP = 8
VMEM_LIMIT_BYTES = 56 * 1024 * 1024

BF16 = jnp.bfloat16
F32 = jnp.float32

_NT = (((1,), (1,)), ((), ()))


def _layer_norm(y, g, b):
    mu = jnp.mean(y, axis=-1, keepdims=True)
    d = y - mu
    var = jnp.mean(d * d, axis=-1, keepdims=True)
    return d * lax.rsqrt(var + LN_EPS) * g + b


def _head_lane_mask(e):
    lane = lax.broadcasted_iota(jnp.int32, (1, LANES), 1)
    return (lane >= e * HEAD_DIM) & (lane < (e + 1) * HEAD_DIM)


def _mem_cross_attention(qm, kv_ref):
    qb = (qm * ATTN_SCALE).astype(BF16)
    v_all = kv_ref[:, W_MEM:]
    lane_v = lax.broadcasted_iota(jnp.int32, (1, W_MEM), 1)
    out = None
    for hm in range(N_MEM_HEADS):
        tile, e = divmod(hm, HEADS_PER_LANE_TILE)
        qp = qb[:, tile * LANES:(tile + 1) * LANES]
        qz = jnp.where(_head_lane_mask(e), qp, jnp.zeros_like(qp))
        kp = kv_ref[:, tile * LANES:(tile + 1) * LANES]
        s = lax.dot_general(qz, kp, _NT, preferred_element_type=F32)
        m = jnp.max(s, axis=-1, keepdims=True)
        p = jnp.exp(s - m)
        l = jnp.sum(p, axis=-1, keepdims=True)
        pn = (p * (1.0 / l)).astype(BF16)
        vz = jnp.where((lane_v >= hm * HEAD_DIM) & (lane_v < (hm + 1) * HEAD_DIM),
                       v_all, jnp.zeros_like(v_all))
        o = jnp.dot(pn, vz, preferred_element_type=F32)
        out = o if out is None else out + o
    return out


def _mem_kv_kernel(mem_ref, w_ref, o_ref):
    o_ref[0] = jnp.dot(mem_ref[...].astype(BF16), w_ref[0].astype(BF16),
                       preferred_element_type=F32).astype(BF16)


def _mem_kv(mem2d, w_mem_kv):
    rows, d = mem2d.shape
    depth, _, n = w_mem_kv.shape
    return pl.pallas_call(
        _mem_kv_kernel,
        grid=(depth,),
        in_specs=[pl.BlockSpec((rows, d), lambda i: (0, 0)),
                  pl.BlockSpec((1, d, n), lambda i: (i, 0, 0))],
        out_specs=pl.BlockSpec((1, rows, n), lambda i: (i, 0, 0)),
        out_shape=jax.ShapeDtypeStruct((depth, rows, n), BF16),
        compiler_params=pltpu.CompilerParams(dimension_semantics=("parallel",),
                                             vmem_limit_bytes=VMEM_LIMIT_BYTES),
        name="mem_kv",
    )(mem2d, w_mem_kv)


def _gmlp_front_kernel(x_ref, w_ref, lng_ref, lnb_ref, ws_ref, bs_ref, kv_ref,
                       mix_ref, mem_ref):
    row = lax.broadcasted_iota(jnp.int32, (GMLP_CHUNK, GMLP_CHUNK), 0)
    col = lax.broadcasted_iota(jnp.int32, (GMLP_CHUNK, GMLP_CHUNK), 1)
    causal = col <= row
    m0 = _head_lane_mask(0)
    m1 = _head_lane_mask(1)
    wcat = []
    for p in range(N_MIX_HEADS // HEADS_PER_LANE_TILE):
        w0 = ws_ref[2 * p]
        w1 = ws_ref[2 * p + 1]
        wcat.append(jnp.concatenate([jnp.where(causal, w0, jnp.zeros_like(w0)),
                                     jnp.where(causal, w1, jnp.zeros_like(w1))], axis=1))

    def sub_tile(r0):
        xb = x_ref[0, r0:r0 + SUB_TILE, :].astype(BF16)
        u = jax.nn.gelu(jnp.dot(xb, w_ref[:, :W_MIX], preferred_element_type=F32))
        gv = jax.nn.gelu(jnp.dot(xb, w_ref[:, W_MIX:2 * W_MIX], preferred_element_type=F32))
        vb = _layer_norm(gv, lng_ref[...], lnb_ref[...]).astype(BF16)
        qm = jnp.dot(xb, w_ref[:, 2 * W_MIX:], preferred_element_type=F32)
        for p in range(N_MIX_HEADS // HEADS_PER_LANE_TILE):
            lanes = slice(p * LANES, (p + 1) * LANES)
            bias = bs_ref[:, lanes]
            for c in range(SUB_TILE // GMLP_CHUNK):
                rows = slice(c * GMLP_CHUNK, (c + 1) * GMLP_CHUNK)
                vp = vb[rows, lanes]
                zero = jnp.zeros_like(vp)
                vbd = jnp.concatenate([jnp.where(m0, vp, zero), jnp.where(m1, vp, zero)], axis=0)
                sv = jnp.dot(wcat[p], vbd, preferred_element_type=F32) + bias
                mix_ref[0, r0 + c * GMLP_CHUNK:r0 + (c + 1) * GMLP_CHUNK, lanes] = (
                    u[rows, lanes] * sv).astype(BF16)
        mem_ref[0, r0:r0 + SUB_TILE, :] = _mem_cross_attention(qm, kv_ref.at[0]).astype(BF16)

    for r0 in range(0, x_ref.shape[1], SUB_TILE):
        sub_tile(r0)


def _gmlp_front(x, w_in, ln_g, ln_b, w_s, b_full, kv):
    b, s, d = x.shape
    t = FRONT_TILE
    n_in = w_in.shape[1]
    return pl.pallas_call(
        _gmlp_front_kernel,
        grid=(b, s // t),
        in_specs=[
            pl.BlockSpec((1, t, d), lambda bi, i: (bi, i, 0)),
            pl.BlockSpec((d, n_in), lambda bi, i: (0, 0)),
            pl.BlockSpec((1, W_MIX), lambda bi, i: (0, 0)),
            pl.BlockSpec((1, W_MIX), lambda bi, i: (0, 0)),
            pl.BlockSpec((N_MIX_HEADS, GMLP_CHUNK, GMLP_CHUNK), lambda bi, i: (0, 0, 0)),
            pl.BlockSpec((GMLP_CHUNK, W_MIX), lambda bi, i: (0, 0)),
            pl.BlockSpec((1, kv.shape[1], kv.shape[2]), lambda bi, i: (bi, 0, 0)),
        ],
        out_specs=[pl.BlockSpec((1, t, W_MIX), lambda bi, i: (bi, i, 0)),
                   pl.BlockSpec((1, t, W_MEM), lambda bi, i: (bi, i, 0))],
        out_shape=[jax.ShapeDtypeStruct((b, s, W_MIX), BF16),
                   jax.ShapeDtypeStruct((b, s, W_MEM), BF16)],
        compiler_params=pltpu.CompilerParams(dimension_semantics=("parallel", "parallel"),
                                             vmem_limit_bytes=VMEM_LIMIT_BYTES),
        name="gmlp_front",
    )(x, w_in, ln_g, ln_b, w_s, b_full, kv)


def _moba_front_kernel(x_ref, wqkm_ref, wvt_ref, kv_ref,
                       q_ref, k_ref, vt_ref, km_ref, mem_ref):
    ones = jnp.ones((V_AUG_ROWS - HEAD_DIM, MOBA_BLOCK), BF16)

    def sub_tile(r0):
        rows = slice(r0, r0 + SUB_TILE)
        xb = x_ref[0, rows, :].astype(BF16)
        q = jnp.dot(xb, wqkm_ref[:, :W_MIX], preferred_element_type=F32)
        q_ref[0, rows, :] = (q * (ATTN_SCALE * LOG2_E)).astype(BF16)
        k = jnp.dot(xb, wqkm_ref[:, W_MIX:2 * W_MIX], preferred_element_type=F32)
        k_ref[0, rows, :] = k.astype(BF16)
        vt = lax.dot_general(wvt_ref[...], xb, _NT, preferred_element_type=F32)
        for c in range(SUB_TILE // MOBA_BLOCK):
            blk = r0 // MOBA_BLOCK + c
            cols = slice(c * MOBA_BLOCK, (c + 1) * MOBA_BLOCK)
            km_ref[0, 0, blk:blk + 1, :] = jnp.mean(k[cols], axis=0, keepdims=True)
            for h in range(N_MIX_HEADS):
                vt_ref[0, blk, h, :HEAD_DIM, :] = vt[h * HEAD_DIM:(h + 1) * HEAD_DIM,
                                                     cols].astype(BF16)
                vt_ref[0, blk, h, HEAD_DIM:, :] = ones
        qm = jnp.dot(xb, wqkm_ref[:, 2 * W_MIX:], preferred_element_type=F32)
        mem_ref[0, rows, :] = _mem_cross_attention(qm, kv_ref.at[0]).astype(BF16)

    for r0 in range(0, x_ref.shape[1], SUB_TILE):
        sub_tile(r0)


def _moba_front(x, w_qkm, w_vt, kv):
    b, s, d = x.shape
    t = FRONT_TILE
    bpt = t // MOBA_BLOCK
    n_blk = s // MOBA_BLOCK
    return pl.pallas_call(
        _moba_front_kernel,
        grid=(b, s // t),
        in_specs=[
            pl.BlockSpec((1, t, d), lambda bi, i: (bi, i, 0)),
            pl.BlockSpec(w_qkm.shape, lambda bi, i: (0, 0)),
            pl.BlockSpec(w_vt.shape, lambda bi, i: (0, 0)),
            pl.BlockSpec((1, kv.shape[1], kv.shape[2]), lambda bi, i: (bi, 0, 0)),
        ],
        out_specs=[
            pl.BlockSpec((1, t, W_MIX), lambda bi, i: (bi, i, 0)),
            pl.BlockSpec((1, t, W_MIX), lambda bi, i: (bi, i, 0)),
            pl.BlockSpec((1, bpt, N_MIX_HEADS, V_AUG_ROWS, MOBA_BLOCK),
                         lambda bi, i: (bi, i, 0, 0, 0)),
            pl.BlockSpec((1, 1, bpt, W_MIX), lambda bi, i: (bi, i, 0, 0)),
            pl.BlockSpec((1, t, W_MEM), lambda bi, i: (bi, i, 0)),
        ],
        out_shape=[
            jax.ShapeDtypeStruct((b, s, W_MIX), BF16),
            jax.ShapeDtypeStruct((b, s, W_MIX), BF16),
            jax.ShapeDtypeStruct((b, n_blk, N_MIX_HEADS, V_AUG_ROWS, MOBA_BLOCK), BF16),
            jax.ShapeDtypeStruct((b, s // t, bpt, W_MIX), F32),
            jax.ShapeDtypeStruct((b, s, W_MEM), BF16),
        ],
        compiler_params=pltpu.CompilerParams(dimension_semantics=("parallel", "parallel"),
                                             vmem_limit_bytes=VMEM_LIMIT_BYTES),
        name="moba_front",
    )(x, w_qkm, w_vt, kv)


def _select_blocks(bs_t, n_past):
    n_blk = bs_t.shape[0]
    nidx = lax.broadcasted_iota(jnp.int32, bs_t.shape, 0)
    rem = nidx < n_past
    sel = jnp.zeros(bs_t.shape, jnp.bool_)
    for _ in range(MOBA_TOPK):
        ms = jnp.where(rem, bs_t, -jnp.inf)
        mx = jnp.max(ms, axis=0, keepdims=True)
        is_max = rem & (ms == mx)
        first = jnp.min(jnp.where(is_max, nidx, n_blk), axis=0, keepdims=True)
        pick = is_max & (nidx == first)
        sel = sel | pick
        rem = rem & jnp.logical_not(pick)
    return sel


def _moba_work_items(n_blk):
    items = []
    for j in range(n_blk):
        own = j // 2
        items.append(j | own << 8 | (j % 2 == 0) << 16 | (j % 2 == 1) << 17 | 1 << 18)
        items.extend(j | i << 8 for i in range(own))
    items.append(items[-1])
    return items


def _moba_attn_kernel(tab_ref, q_ref, k_ref, vt_ref, km_ref, o_ref,
                      qz_ref, bias_ref, cz_ref, s_ref, p_ref, sums_ref):
    nq = MOBA_BLOCK
    heads = range(HEADS_PER_LANE_TILE)
    width = HEADS_PER_LANE_TILE * nq
    n_blk = bias_ref.shape[0]
    n_items = tab_ref.shape[0] - 1

    def head_cols(a, e):
        return a[:, e * nq:(e + 1) * nq]

    def query_rows(tile):
        return pl.ds(pl.multiple_of(tile * nq, nq), nq)

    krow = lax.broadcasted_iota(jnp.int32, (nq, width), 0)
    qcol = lax.broadcasted_iota(jnp.int32, (nq, width), 1) & (nq - 1)
    cz_ref[0] = jnp.zeros((nq, width), F32)
    cz_ref[1] = jnp.where(krow <= qcol, 0.0, -jnp.inf).astype(F32)

    kmb = km_ref[0].astype(BF16)

    def select(rows, j, carry):
        qp = q_ref[0, query_rows(j), :]
        zero = jnp.zeros_like(qp)
        qz = jnp.concatenate([jnp.where(_head_lane_mask(e), qp, zero) for e in heads], axis=0)
        qz_t = qz.T
        qz_ref[j] = qz_t
        bs_t = jnp.dot(kmb[:rows], qz_t, preferred_element_type=F32)
        nidx = lax.broadcasted_iota(jnp.int32, (rows, width), 0)
        keep = _select_blocks(bs_t, j) | (nidx == j)
        bias_ref[j, :rows] = jnp.where(keep, 0.0, -jnp.inf).astype(F32)
        if rows < n_blk:
            bias_ref[j, rows:] = jnp.full((n_blk - rows, width), -jnp.inf, F32)
        return carry

    for j0 in range(0, n_blk, SELECT_GROUP):
        lax.fori_loop(j0, j0 + SELECT_GROUP, functools.partial(select, j0 + SELECT_GROUP), 0,
                      unroll=4)

    def decode(w):
        code = tab_ref[w]
        return code & 0xFF, (code >> 8) & 0xFF, (code >> 16) & 1, (code >> 17) & 1, (code >> 18) & 1

    def block_bias(w):
        tile, pair = decode(w)[:2]
        return bias_ref[tile, pl.ds(2 * pair, 1), :], bias_ref[tile, pl.ds(2 * pair + 1, 1), :]

    def masked_scores(w, s_buf):
        tile, pair, causal_lo, causal_hi, _ = decode(w)
        s = jnp.dot(k_ref[0, pair], qz_ref[tile], preferred_element_type=F32)
        s0 = s[:nq] + cz_ref[causal_lo]
        s1 = s[nq:] + cz_ref[causal_hi]
        s_buf[:nq] = s0
        s_buf[nq:] = s1
        b0, b1 = block_bias(w)
        return jnp.maximum(jnp.max(s0, axis=0, keepdims=True) + b0,
                           jnp.max(s1, axis=0, keepdims=True) + b1)

    def probabilities(w, m, cmax, s_buf):
        first = decode(w)[4] == 1
        m = jnp.where(first, jnp.full_like(m, RUNNING_MAX_FLOOR), m)
        m_new = jnp.maximum(m, cmax)
        b0, b1 = block_bias(w)
        p0 = jnp.exp2(s_buf[:nq] - (m_new - b0)).astype(BF16)
        p1 = jnp.exp2(s_buf[nq:] - (m_new - b1)).astype(BF16)
        for e in heads:
            p_ref[e, :nq] = head_cols(p0, e)
            p_ref[e, nq:] = head_cols(p1, e)
        return m_new, jnp.where(first, jnp.zeros_like(m), jnp.exp2(m - m_new))

    def accumulate(w, alpha, acc):
        tile, pair = decode(w)[:2]
        acc = tuple(
            head_cols(alpha, e) * acc[e]
            + (jnp.dot(vt_ref[0, 2 * pair, e], p_ref[e, :nq], preferred_element_type=F32)
               + jnp.dot(vt_ref[0, 2 * pair + 1, e], p_ref[e, nq:], preferred_element_type=F32))
            for e in heads)
        for e in heads:
            sums_ref[tile, e] = acc[e]
        return acc

    def one_item(w, carry):
        m, acc, cmax, alpha = carry
        acc = accumulate(jnp.maximum(w - 1, 0), alpha, acc)
        m, alpha = probabilities(w, m, cmax, s_ref)
        return m, acc, masked_scores(w + 1, s_ref), alpha

    p_ref[...] = jnp.zeros(p_ref.shape, BF16)
    carry = (jnp.zeros((1, width), F32), tuple(jnp.ones((V_AUG_ROWS, nq), F32) for _ in heads),
             masked_scores(0, s_ref), jnp.ones((1, width), F32))
    m, acc, _, alpha = lax.fori_loop(0, n_items, one_item, carry, unroll=2)
    accumulate(n_items - 1, alpha, acc)

    def normalise(j, carry):
        o_t = jnp.concatenate(
            [sums_ref[j, e, :HEAD_DIM] / sums_ref[j, e, HEAD_DIM:HEAD_DIM + 1] for e in heads],
            axis=0)
        o_ref[0, query_rows(j), :] = o_t.T.astype(BF16)
        return carry

    lax.fori_loop(0, n_blk, normalise, 0, unroll=4)


def _moba_attn(q, k_pairs, vt, km):
    b, s, _ = q.shape
    n_blk = s // MOBA_BLOCK
    n_tiles = W_MIX // LANES
    width = HEADS_PER_LANE_TILE * MOBA_BLOCK
    table = jnp.asarray(_moba_work_items(n_blk), jnp.int32)
    grid_spec = pltpu.PrefetchScalarGridSpec(
        num_scalar_prefetch=1,
        grid=(b, n_tiles),
        in_specs=[
            pl.BlockSpec((1, s, LANES), lambda bi, p, tab: (bi, 0, p)),
            pl.BlockSpec((1, n_blk // 2, 2 * MOBA_BLOCK, LANES), lambda bi, p, tab: (bi, 0, 0, p)),
            pl.BlockSpec((1, n_blk, HEADS_PER_LANE_TILE, V_AUG_ROWS, MOBA_BLOCK),
                         lambda bi, p, tab: (bi, 0, p, 0, 0)),
            pl.BlockSpec((1, n_blk, LANES), lambda bi, p, tab: (bi, 0, p)),
        ],
        out_specs=pl.BlockSpec((1, s, LANES), lambda bi, p, tab: (bi, 0, p)),
        scratch_shapes=[
            pltpu.VMEM((n_blk, LANES, width), BF16),
            pltpu.VMEM((n_blk, n_blk, width), F32),
            pltpu.VMEM((2, MOBA_BLOCK, width), F32),
            pltpu.VMEM((2 * MOBA_BLOCK, width), F32),
            pltpu.VMEM((HEADS_PER_LANE_TILE, 2 * MOBA_BLOCK, MOBA_BLOCK), BF16),
            pltpu.VMEM((n_blk, HEADS_PER_LANE_TILE, V_AUG_ROWS, MOBA_BLOCK), F32),
        ],
    )
    return pl.pallas_call(
        _moba_attn_kernel,
        grid_spec=grid_spec,
        out_shape=jax.ShapeDtypeStruct((b, s, W_MIX), BF16),
        compiler_params=pltpu.CompilerParams(
            dimension_semantics=("parallel", "parallel"),
            vmem_limit_bytes=VMEM_LIMIT_BYTES),
        name="moba_attn",
    )(table, q, k_pairs, vt, km)


def _post_mixer_kernel(x_ref, mix_ref, mem_ref, wo_ref, g1_ref, b1_ref,
                       wi_ref, wd_ref, g2_ref, b2_ref, o_ref, act_ref):
    d_ff = wd_ref.shape[1]

    rows_a = slice(0, SUB_TILE)
    rows_b = slice(SUB_TILE, 2 * SUB_TILE)

    def out_proj(rows):
        return (jnp.dot(mix_ref[0, rows, :], wo_ref[0, :W_MIX, :], preferred_element_type=F32)
                + jnp.dot(mem_ref[0, rows, :], wo_ref[0, W_MIX:, :], preferred_element_type=F32))

    def first_norm(rows, sub):
        return _layer_norm(ALPHA * x_ref[0, rows, :] + sub, g1_ref[0], b1_ref[0])

    def ffn_in(rows, x1b, chunks):
        for c in chunks:
            cols = slice(c * FFN_CHUNK, (c + 1) * FFN_CHUNK)
            up_cols = slice(d_ff + c * FFN_CHUNK, d_ff + (c + 1) * FFN_CHUNK)
            g = jnp.dot(x1b, wi_ref[0, :, cols], preferred_element_type=F32)
            u = jnp.dot(x1b, wi_ref[0, :, up_cols], preferred_element_type=F32)
            act_ref[rows, cols] = (jax.nn.silu(g) * u).astype(BF16)

    def ffn_out(rows):
        return jnp.dot(act_ref[rows, :], wd_ref[0], preferred_element_type=F32)

    def last_norm(rows, x1, y):
        o_ref[0, rows, :] = _layer_norm(ALPHA * x1 + y, g2_ref[0], b2_ref[0])

    n_chunks = d_ff // FFN_CHUNK
    sub_a = out_proj(rows_a)
    sub_b = out_proj(rows_b)
    x1_a = first_norm(rows_a, sub_a)
    x1b_a = x1_a.astype(BF16)
    ffn_in(rows_a, x1b_a, range(0, 2))
    x1_b = first_norm(rows_b, sub_b)
    x1b_b = x1_b.astype(BF16)
    ffn_in(rows_a, x1b_a, range(2, n_chunks))
    ffn_in(rows_b, x1b_b, range(0, n_chunks))
    y_a = ffn_out(rows_a)
    y_b = ffn_out(rows_b)
    last_norm(rows_a, x1_a, y_a)
    last_norm(rows_b, x1_b, y_b)


def _post_mixer(layer, x, mix, mem_out, wo, g1, b1, wi, wd, g2, b2):
    b, s, d = x.shape
    t = TOKEN_TILE
    d_ff = wd.shape[1]
    single = pl.Buffered(1)

    def wspec(arr):
        return pl.BlockSpec((1,) + arr.shape[1:], lambda bi, i: (layer,) + (0,) * (arr.ndim - 1),
                            pipeline_mode=single)

    return pl.pallas_call(
        _post_mixer_kernel,
        grid=(b, s // t),
        in_specs=[
            pl.BlockSpec((1, t, d), lambda bi, i: (bi, i, 0)),
            pl.BlockSpec((1, t, W_MIX), lambda bi, i: (bi, i, 0)),
            pl.BlockSpec((1, t, W_MEM), lambda bi, i: (bi, i, 0)),
            wspec(wo), wspec(g1), wspec(b1), wspec(wi), wspec(wd), wspec(g2), wspec(b2),
        ],
        out_specs=pl.BlockSpec((1, t, d), lambda bi, i: (bi, i, 0)),
        out_shape=jax.ShapeDtypeStruct((b, s, d), F32),
        scratch_shapes=[pltpu.VMEM((t, d_ff), BF16)],
        compiler_params=pltpu.CompilerParams(dimension_semantics=("parallel", "parallel"),
                                             vmem_limit_bytes=VMEM_LIMIT_BYTES),
        name="post_mixer",
    )(x, mix, mem_out, wo, g1, b1, wi, wd, g2, b2)


def kernel(x, mem, a_w_in, a_ln_v_g, a_ln_v_b, a_w_s, a_b_s, b_w_in, w_mem_kv, w_mix_out,
           ln_mix_g, ln_mix_b, w_ffn_in, w_ffn_out, ln_ffn_g, ln_ffn_b):
    b, s, d = x.shape
    n_mem = mem.shape[1]
    d_ff = w_ffn_out.shape[1]
    assert s % FRONT_TILE == 0 and FRONT_TILE % SUB_TILE == 0 and SUB_TILE % MOBA_BLOCK == 0
    assert s % TOKEN_TILE == 0 and TOKEN_TILE == 2 * SUB_TILE and d_ff % FFN_CHUNK == 0
    n_blk = s // MOBA_BLOCK
    assert n_blk % 2 == 0 and n_blk % SELECT_GROUP == 0

    kv = _mem_kv(mem.reshape(b * n_mem, d), w_mem_kv).reshape(DEPTH, b, n_mem, 2 * W_MEM)

    wo, wi, wd = (w.astype(BF16) for w in (w_mix_out, w_ffn_in, w_ffn_out))
    ln_rows = [v.reshape(DEPTH, 1, d) for v in (ln_mix_g, ln_mix_b, ln_ffn_g, ln_ffn_b)]

    def post(i, xin, mix, mem_out):
        return _post_mixer(i, xin, mix, mem_out, wo, ln_rows[0], ln_rows[1], wi, wd,
                           ln_rows[2], ln_rows[3])

    b_full = jnp.repeat(a_b_s[0].T, HEAD_DIM, axis=1)
    mix, mem_out = _gmlp_front(x, a_w_in[0].astype(BF16), a_ln_v_g[0].reshape(1, -1),
                               a_ln_v_b[0].reshape(1, -1), a_w_s[0].astype(BF16), b_full, kv[0])
    x = post(0, x, mix, mem_out)

    w1 = b_w_in[0].astype(BF16)
    w_qkm = jnp.concatenate([w1[:, :2 * W_MIX], w1[:, 3 * W_MIX:]], axis=1)
    w_vt = w1[:, 2 * W_MIX:3 * W_MIX].T
    q, k, vt, km, mem_out = _moba_front(x, w_qkm, w_vt, kv[1])
    mix = _moba_attn(q, k.reshape(b, n_blk // 2, 2 * MOBA_BLOCK, W_MIX), vt,
                     km.reshape(b, n_blk, W_MIX))
    return post(1, x, mix, mem_out)
```

```python
import functools

import jax
import jax.numpy as jnp
from jax import lax
from jax.experimental import pallas as pl
from jax.experimental.pallas import tpu as pltpu

HEAD_DIM = 64
N_MIX_HEADS = 12
W_MIX = N_MIX_HEADS * HEAD_DIM
N_MEM_HEADS = 4
W_MEM = N_MEM_HEADS * HEAD_DIM
GMLP_CHUNK = 128
MOBA_BLOCK = 256
MOBA_TOPK = 3
DEPTH = 2
ALPHA = (2.0 * DEPTH) ** 0.25
LN_EPS = 1e-5
ATTN_SCALE = HEAD_DIM ** -0.5
LOG2_E = 1.4426950408889634
RUNNING_MAX_FLOOR = -1e30
BF16_SUBLANES = 16
V_AUG_ROWS = HEAD_DIM + BF16_SUBLANES

LANES = 128
HEADS_PER_LANE_TILE = LANES // HEAD_DIM
TOKEN_TILE = 1024
FRONT_TILE = 1024
SUB_TILE = 512
FFN_CHUNK = 256
SELECT_GROUP = 8
VMEM_LIMIT_BYTES = 56 * 1024 * 1024

BF16 = jnp.bfloat16
F32 = jnp.float32

_NT = (((1,), (1,)), ((), ()))


def _layer_norm(y, g, b):
    mu = jnp.mean(y, axis=-1, keepdims=True)
    d = y - mu
    var = jnp.mean(d * d, axis=-1, keepdims=True)
    return d * lax.rsqrt(var + LN_EPS) * g + b


def _head_lane_mask(e):
    lane = lax.broadcasted_iota(jnp.int32, (1, LANES), 1)
    return (lane >= e * HEAD_DIM) & (lane < (e + 1) * HEAD_DIM)


def _mem_cross_attention(qm, kv_ref):
    qb = (qm * ATTN_SCALE).astype(BF16)
    v_all = kv_ref[:, W_MEM:]
    lane_v = lax.broadcasted_iota(jnp.int32, (1, W_MEM), 1)
    out = None
    for hm in range(N_MEM_HEADS):
        tile, e = divmod(hm, HEADS_PER_LANE_TILE)
        qp = qb[:, tile * LANES:(tile + 1) * LANES]
        qz = jnp.where(_head_lane_mask(e), qp, jnp.zeros_like(qp))
        kp = kv_ref[:, tile * LANES:(tile + 1) * LANES]
        s = lax.dot_general(qz, kp, _NT, preferred_element_type=F32)
        m = jnp.max(s, axis=-1, keepdims=True)
        p = jnp.exp(s - m)
        l = jnp.sum(p, axis=-1, keepdims=True)
        pn = (p * (1.0 / l)).astype(BF16)
        vz = jnp.where((lane_v >= hm * HEAD_DIM) & (lane_v < (hm + 1) * HEAD_DIM),
                       v_all, jnp.zeros_like(v_all))
        o = jnp.dot(pn, vz, preferred_element_type=F32)
        out = o if out is None else out + o
    return out


def _mem_kv_kernel(mem_ref, w_ref, o_ref):
    o_ref[0] = jnp.dot(mem_ref[...].astype(BF16), w_ref[0].astype(BF16),
                       preferred_element_type=F32).astype(BF16)


def _mem_kv(mem2d, w_mem_kv):
    rows, d = mem2d.shape
    depth, _, n = w_mem_kv.shape
    return pl.pallas_call(
        _mem_kv_kernel,
        grid=(depth,),
        in_specs=[pl.BlockSpec((rows, d), lambda i: (0, 0)),
                  pl.BlockSpec((1, d, n), lambda i: (i, 0, 0))],
        out_specs=pl.BlockSpec((1, rows, n), lambda i: (i, 0, 0)),
        out_shape=jax.ShapeDtypeStruct((depth, rows, n), BF16),
        compiler_params=pltpu.CompilerParams(dimension_semantics=("parallel",),
                                             vmem_limit_bytes=VMEM_LIMIT_BYTES),
        name="mem_kv",
    )(mem2d, w_mem_kv)


def _gmlp_front_kernel(x_ref, w_ref, lng_ref, lnb_ref, ws_ref, bs_ref, kv_ref,
                       mix_ref, mem_ref):
    row = lax.broadcasted_iota(jnp.int32, (GMLP_CHUNK, GMLP_CHUNK), 0)
    col = lax.broadcasted_iota(jnp.int32, (GMLP_CHUNK, GMLP_CHUNK), 1)
    causal = col <= row
    m0 = _head_lane_mask(0)
    m1 = _head_lane_mask(1)
    wcat = []
    for p in range(N_MIX_HEADS // HEADS_PER_LANE_TILE):
        w0 = ws_ref[2 * p]
        w1 = ws_ref[2 * p + 1]
        wcat.append(jnp.concatenate([jnp.where(causal, w0, jnp.zeros_like(w0)),
                                     jnp.where(causal, w1, jnp.zeros_like(w1))], axis=1))

    def project(r0):
        xb = x_ref[0, r0:r0 + SUB_TILE, :].astype(BF16)
        return (jnp.dot(xb, w_ref[:, :W_MIX], preferred_element_type=F32),
                jnp.dot(xb, w_ref[:, W_MIX:2 * W_MIX], preferred_element_type=F32),
                jnp.dot(xb, w_ref[:, 2 * W_MIX:], preferred_element_type=F32))

    def mix(r0, hu, hv, qm):
        u = jax.nn.gelu(hu)
        vb = _layer_norm(jax.nn.gelu(hv), lng_ref[...], lnb_ref[...]).astype(BF16)
        for p in range(N_MIX_HEADS // HEADS_PER_LANE_TILE):
            lanes = slice(p * LANES, (p + 1) * LANES)
            bias = bs_ref[:, lanes]
            for c in range(SUB_TILE // GMLP_CHUNK):
                rows = slice(c * GMLP_CHUNK, (c + 1) * GMLP_CHUNK)
                vp = vb[rows, lanes]
                zero = jnp.zeros_like(vp)
                vbd = jnp.concatenate([jnp.where(m0, vp, zero), jnp.where(m1, vp, zero)], axis=0)
                sv = jnp.dot(wcat[p], vbd, preferred_element_type=F32) + bias
                mix_ref[0, r0 + c * GMLP_CHUNK:r0 + (c + 1) * GMLP_CHUNK, lanes] = (
                    u[rows, lanes] * sv).astype(BF16)
        mem_ref[0, r0:r0 + SUB_TILE, :] = _mem_cross_attention(qm, kv_ref.at[0]).astype(BF16)

    starts = range(0, x_ref.shape[1], SUB_TILE)
    projected = [project(r0) for r0 in starts]
    for r0, h in zip(starts, projected):
        mix(r0, *h)


def _gmlp_front(x, w_in, ln_g, ln_b, w_s, b_full, kv):
    b, s, d = x.shape
    t = FRONT_TILE
    n_in = w_in.shape[1]
    return pl.pallas_call(
        _gmlp_front_kernel,
        grid=(b, s // t),
        in_specs=[
            pl.BlockSpec((1, t, d), lambda bi, i: (bi, i, 0)),
            pl.BlockSpec((d, n_in), lambda bi, i: (0, 0)),
            pl.BlockSpec((1, W_MIX), lambda bi, i: (0, 0)),
            pl.BlockSpec((1, W_MIX), lambda bi, i: (0, 0)),
            pl.BlockSpec((N_MIX_HEADS, GMLP_CHUNK, GMLP_CHUNK), lambda bi, i: (0, 0, 0)),
            pl.BlockSpec((GMLP_CHUNK, W_MIX), lambda bi, i: (0, 0)),
            pl.BlockSpec((1, kv.shape[1], kv.shape[2]), lambda bi, i: (bi, 0, 0)),
        ],
        out_specs=[pl.BlockSpec((1, t, W_MIX), lambda bi, i: (bi, i, 0)),
                   pl.BlockSpec((1, t, W_MEM), lambda bi, i: (bi, i, 0))],
        out_shape=[jax.ShapeDtypeStruct((b, s, W_MIX), BF16),
                   jax.ShapeDtypeStruct((b, s, W_MEM), BF16)],
        compiler_params=pltpu.CompilerParams(dimension_semantics=("parallel", "parallel"),
                                             vmem_limit_bytes=VMEM_LIMIT_BYTES),
        name="gmlp_front",
    )(x, w_in, ln_g, ln_b, w_s, b_full, kv)


def _moba_front_kernel(x_ref, wqkm_ref, wvt_ref, kv_ref,
                       q_ref, k_ref, vt_ref, km_ref, mem_ref):
    ones = jnp.ones((V_AUG_ROWS - HEAD_DIM, MOBA_BLOCK), BF16)

    def sub_tile(r0):
        rows = slice(r0, r0 + SUB_TILE)
        xb = x_ref[0, rows, :].astype(BF16)
        q = jnp.dot(xb, wqkm_ref[:, :W_MIX], preferred_element_type=F32)
        q_ref[0, rows, :] = (q * (ATTN_SCALE * LOG2_E)).astype(BF16)
        k = jnp.dot(xb, wqkm_ref[:, W_MIX:2 * W_MIX], preferred_element_type=F32)
        k_ref[0, rows, :] = k.astype(BF16)
        vt = lax.dot_general(wvt_ref[...], xb, _NT, preferred_element_type=F32)
        for c in range(SUB_TILE // MOBA_BLOCK):
            blk = r0 // MOBA_BLOCK + c
            cols = slice(c * MOBA_BLOCK, (c + 1) * MOBA_BLOCK)
            km_ref[0, 0, blk:blk + 1, :] = jnp.mean(k[cols], axis=0, keepdims=True)
            for h in range(N_MIX_HEADS):
                vt_ref[0, blk, h, :HEAD_DIM, :] = vt[h * HEAD_DIM:(h + 1) * HEAD_DIM,
                                                     cols].astype(BF16)
                vt_ref[0, blk, h, HEAD_DIM:, :] = ones
        return jnp.dot(xb, wqkm_ref[:, 2 * W_MIX:], preferred_element_type=F32)

    starts = range(0, x_ref.shape[1], SUB_TILE)
    mem_queries = [sub_tile(r0) for r0 in starts]
    for r0, qm in zip(starts, mem_queries):
        mem_ref[0, r0:r0 + SUB_TILE, :] = _mem_cross_attention(qm, kv_ref.at[0]).astype(BF16)


def _moba_front(x, w_qkm, w_vt, kv):
    b, s, d = x.shape
    t = FRONT_TILE
    bpt = t // MOBA_BLOCK
    n_blk = s // MOBA_BLOCK
    return pl.pallas_call(
        _moba_front_kernel,
        grid=(b, s // t),
        in_specs=[
            pl.BlockSpec((1, t, d), lambda bi, i: (bi, i, 0)),
            pl.BlockSpec(w_qkm.shape, lambda bi, i: (0, 0)),
            pl.BlockSpec(w_vt.shape, lambda bi, i: (0, 0)),
            pl.BlockSpec((1, kv.shape[1], kv.shape[2]), lambda bi, i: (bi, 0, 0)),
        ],
        out_specs=[
            pl.BlockSpec((1, t, W_MIX), lambda bi, i: (bi, i, 0)),
            pl.BlockSpec((1, t, W_MIX), lambda bi, i: (bi, i, 0)),
            pl.BlockSpec((1, bpt, N_MIX_HEADS, V_AUG_ROWS, MOBA_BLOCK),
                         lambda bi, i: (bi, i, 0, 0, 0)),
            pl.BlockSpec((1, 1, bpt, W_MIX), lambda bi, i: (bi, i, 0, 0)),
            pl.BlockSpec((1, t, W_MEM), lambda bi, i: (bi, i, 0)),
        ],
        out_shape=[
            jax.ShapeDtypeStruct((b, s, W_MIX), BF16),
            jax.ShapeDtypeStruct((b, s, W_MIX), BF16),
            jax.ShapeDtypeStruct((b, n_blk, N_MIX_HEADS, V_AUG_ROWS, MOBA_BLOCK), BF16),
            jax.ShapeDtypeStruct((b, s // t, bpt, W_MIX), F32),
            jax.ShapeDtypeStruct((b, s, W_MEM), BF16),
        ],
        compiler_params=pltpu.CompilerParams(dimension_semantics=("parallel", "parallel"),
                                             vmem_limit_bytes=VMEM_LIMIT_BYTES),
        name="moba_front",
    )(x, w_qkm, w_vt, kv)


def _select_blocks(bs_t, n_past):
    n_blk = bs_t.shape[0]
    nidx = lax.broadcasted_iota(jnp.int32, bs_t.shape, 0)
    rem = nidx < n_past
    sel = jnp.zeros(bs_t.shape, jnp.bool_)
    for _ in range(MOBA_TOPK):
        ms = jnp.where(rem, bs_t, -jnp.inf)
        mx = jnp.max(ms, axis=0, keepdims=True)
        is_max = rem & (ms == mx)
        first = jnp.min(jnp.where(is_max, nidx, n_blk), axis=0, keepdims=True)
        pick = is_max & (nidx == first)
        sel = sel | pick
        rem = rem & jnp.logical_not(pick)
    return sel


def _moba_work_items(n_blk):
    items = []
    for j in range(n_blk):
        own = j // 2
        items.append(j | own << 8 | (j % 2 == 0) << 16 | (j % 2 == 1) << 17 | 1 << 18)
        items.extend(j | i << 8 for i in range(own))
    items.append(items[-1])
    return items


def _moba_attn_kernel(tab_ref, q_ref, k_ref, vt_ref, km_ref, o_ref,
                      qz_ref, bias_ref, cz_ref, s_ref, p_ref, sums_ref):
    nq = MOBA_BLOCK
    heads = range(HEADS_PER_LANE_TILE)
    width = HEADS_PER_LANE_TILE * nq
    n_blk = bias_ref.shape[0]
    n_items = tab_ref.shape[0] - 1

    def head_cols(a, e):
        return a[:, e * nq:(e + 1) * nq]

    def query_rows(tile):
        return pl.ds(pl.multiple_of(tile * nq, nq), nq)

    krow = lax.broadcasted_iota(jnp.int32, (nq, width), 0)
    qcol = lax.broadcasted_iota(jnp.int32, (nq, width), 1) & (nq - 1)
    cz_ref[0] = jnp.zeros((nq, width), F32)
    cz_ref[1] = jnp.where(krow <= qcol, 0.0, -jnp.inf).astype(F32)

    kmb = km_ref[0].astype(BF16)

    def select(rows, j, carry):
        qp = q_ref[0, query_rows(j), :]
        zero = jnp.zeros_like(qp)
        qz = jnp.concatenate([jnp.where(_head_lane_mask(e), qp, zero) for e in heads], axis=0)
        qz_t = qz.T
        qz_ref[j] = qz_t
        bs_t = jnp.dot(kmb[:rows], qz_t, preferred_element_type=F32)
        nidx = lax.broadcasted_iota(jnp.int32, (rows, width), 0)
        keep = _select_blocks(bs_t, j) | (nidx == j)
        bias_ref[j, :rows] = jnp.where(keep, 0.0, -jnp.inf).astype(F32)
        if rows < n_blk:
            bias_ref[j, rows:] = jnp.full((n_blk - rows, width), -jnp.inf, F32)
        return carry

    for j0 in range(0, n_blk, SELECT_GROUP):
        lax.fori_loop(j0, j0 + SELECT_GROUP, functools.partial(select, j0 + SELECT_GROUP), 0,
                      unroll=4)

    def decode(w):
        code = tab_ref[w]
        return code & 0xFF, (code >> 8) & 0xFF, (code >> 16) & 1, (code >> 17) & 1, (code >> 18) & 1

    def block_bias(w):
        tile, pair = decode(w)[:2]
        return bias_ref[tile, pl.ds(2 * pair, 1), :], bias_ref[tile, pl.ds(2 * pair + 1, 1), :]

    def masked_scores(w, s_buf):
        tile, pair, causal_lo, causal_hi, _ = decode(w)
        s = jnp.dot(k_ref[0, pair], qz_ref[tile], preferred_element_type=F32)
        s0 = s[:nq] + cz_ref[causal_lo]
        s1 = s[nq:] + cz_ref[causal_hi]
        s_buf[:nq] = s0
        s_buf[nq:] = s1
        b0, b1 = block_bias(w)
        return jnp.maximum(jnp.max(s0, axis=0, keepdims=True) + b0,
                           jnp.max(s1, axis=0, keepdims=True) + b1)

    def probabilities(w, m, cmax, s_buf):
        first = decode(w)[4] == 1
        m = jnp.where(first, jnp.full_like(m, RUNNING_MAX_FLOOR), m)
        m_new = jnp.maximum(m, cmax)
        b0, b1 = block_bias(w)
        p0 = jnp.exp2(s_buf[:nq] - (m_new - b0)).astype(BF16)
        p1 = jnp.exp2(s_buf[nq:] - (m_new - b1)).astype(BF16)
        for e in heads:
            p_ref[e, :nq] = head_cols(p0, e)
            p_ref[e, nq:] = head_cols(p1, e)
        return m_new, jnp.where(first, jnp.zeros_like(m), jnp.exp2(m - m_new))

    def accumulate(w, alpha, acc):
        tile, pair = decode(w)[:2]
        acc = tuple(
            head_cols(alpha, e) * acc[e]
            + (jnp.dot(vt_ref[0, 2 * pair, e], p_ref[e, :nq], preferred_element_type=F32)
               + jnp.dot(vt_ref[0, 2 * pair + 1, e], p_ref[e, nq:], preferred_element_type=F32))
            for e in heads)
        for e in heads:
            sums_ref[tile, e] = acc[e]
        return acc

    def one_item(w, carry):
        m, acc, cmax, alpha = carry
        acc = accumulate(jnp.maximum(w - 1, 0), alpha, acc)
        m, alpha = probabilities(w, m, cmax, s_ref)
        return m, acc, masked_scores(w + 1, s_ref), alpha

    p_ref[...] = jnp.zeros(p_ref.shape, BF16)
    carry = (jnp.zeros((1, width), F32), tuple(jnp.ones((V_AUG_ROWS, nq), F32) for _ in heads),
             masked_scores(0, s_ref), jnp.ones((1, width), F32))
    m, acc, _, alpha = lax.fori_loop(0, n_items, one_item, carry, unroll=2)
    accumulate(n_items - 1, alpha, acc)

    def normalise(j, carry):
        o_t = jnp.concatenate(
            [sums_ref[j, e, :HEAD_DIM] / sums_ref[j, e, HEAD_DIM:HEAD_DIM + 1] for e in heads],
            axis=0)
        o_ref[0, query_rows(j), :] = o_t.T.astype(BF16)
        return carry

    lax.fori_loop(0, n_blk, normalise, 0, unroll=4)


def _moba_attn(q, k_pairs, vt, km):
    b, s, _ = q.shape
    n_blk = s // MOBA_BLOCK
    n_tiles = W_MIX // LANES
    width = HEADS_PER_LANE_TILE * MOBA_BLOCK
    table = jnp.asarray(_moba_work_items(n_blk), jnp.int32)
    grid_spec = pltpu.PrefetchScalarGridSpec(
        num_scalar_prefetch=1,
        grid=(b, n_tiles),
        in_specs=[
            pl.BlockSpec((1, s, LANES), lambda bi, p, tab: (bi, 0, p)),
            pl.BlockSpec((1, n_blk // 2, 2 * MOBA_BLOCK, LANES), lambda bi, p, tab: (bi, 0, 0, p)),
            pl.BlockSpec((1, n_blk, HEADS_PER_LANE_TILE, V_AUG_ROWS, MOBA_BLOCK),
                         lambda bi, p, tab: (bi, 0, p, 0, 0)),
            pl.BlockSpec((1, n_blk, LANES), lambda bi, p, tab: (bi, 0, p)),
        ],
        out_specs=pl.BlockSpec((1, s, LANES), lambda bi, p, tab: (bi, 0, p)),
        scratch_shapes=[
            pltpu.VMEM((n_blk, LANES, width), BF16),
            pltpu.VMEM((n_blk, n_blk, width), F32),
            pltpu.VMEM((2, MOBA_BLOCK, width), F32),
            pltpu.VMEM((2 * MOBA_BLOCK, width), F32),
            pltpu.VMEM((HEADS_PER_LANE_TILE, 2 * MOBA_BLOCK, MOBA_BLOCK), BF16),
            pltpu.VMEM((n_blk, HEADS_PER_LANE_TILE, V_AUG_ROWS, MOBA_BLOCK), F32),
        ],
    )
    return pl.pallas_call(
        _moba_attn_kernel,
        grid_spec=grid_spec,
        out_shape=jax.ShapeDtypeStruct((b, s, W_MIX), BF16),
        compiler_params=pltpu.CompilerParams(
            dimension_semantics=("parallel", "parallel"),
            vmem_limit_bytes=VMEM_LIMIT_BYTES),
        name="moba_attn",
    )(table, q, k_pairs, vt, km)


def _post_mixer_kernel(x_ref, mix_ref, mem_ref, wo_ref, g1_ref, b1_ref,
                       wi_ref, wd_ref, g2_ref, b2_ref, o_ref, act_ref):
    d_ff = wd_ref.shape[1]

    rows_a = slice(0, SUB_TILE)
    rows_b = slice(SUB_TILE, 2 * SUB_TILE)

    def out_proj(rows):
        return (jnp.dot(mix_ref[0, rows, :], wo_ref[0, :W_MIX, :], preferred_element_type=F32)
                + jnp.dot(mem_ref[0, rows, :], wo_ref[0, W_MIX:, :], preferred_element_type=F32))

    def first_norm(rows, sub):
        return _layer_norm(ALPHA * x_ref[0, rows, :] + sub, g1_ref[0], b1_ref[0])

    def ffn_in(rows, x1b, chunks):
        for c in chunks:
            cols = slice(c * FFN_CHUNK, (c + 1) * FFN_CHUNK)
            up_cols = slice(d_ff + c * FFN_CHUNK, d_ff + (c + 1) * FFN_CHUNK)
            g = jnp.dot(x1b, wi_ref[0, :, cols], preferred_element_type=F32)
            u = jnp.dot(x1b, wi_ref[0, :, up_cols], preferred_element_type=F32)
            act_ref[rows, cols] = (jax.nn.silu(g) * u).astype(BF16)

    def ffn_out(rows):
        return jnp.dot(act_ref[rows, :], wd_ref[0], preferred_element_type=F32)

    def last_norm(rows, x1, y):
        o_ref[0, rows, :] = _layer_norm(ALPHA * x1 + y, g2_ref[0], b2_ref[0])

    n_chunks = d_ff // FFN_CHUNK
    sub_a = out_proj(rows_a)
    sub_b = out_proj(rows_b)
    x1_a = first_norm(rows_a, sub_a)
    x1b_a = x1_a.astype(BF16)
    ffn_in(rows_a, x1b_a, range(0, 2))
    x1_b = first_norm(rows_b, sub_b)
    x1b_b = x1_b.astype(BF16)
    ffn_in(rows_a, x1b_a, range(2, n_chunks))
    ffn_in(rows_b, x1b_b, range(0, n_chunks))
    y_a = ffn_out(rows_a)
    y_b = ffn_out(rows_b)
    last_norm(rows_a, x1_a, y_a)
    last_norm(rows_b, x1_b, y_b)


def _post_mixer(layer, x, mix, mem_out, wo, g1, b1, wi, wd, g2, b2):
    b, s, d = x.shape
    t = TOKEN_TILE
    d_ff = wd.shape[1]
    single = pl.Buffered(1)

    def wspec(arr):
        return pl.BlockSpec((1,) + arr.shape[1:], lambda bi, i: (layer,) + (0,) * (arr.ndim - 1),
                            pipeline_mode=single)

    return pl.pallas_call(
        _post_mixer_kernel,
        grid=(b, s // t),
        in_specs=[
            pl.BlockSpec((1, t, d), lambda bi, i: (bi, i, 0)),
            pl.BlockSpec((1, t, W_MIX), lambda bi, i: (bi, i, 0)),
            pl.BlockSpec((1, t, W_MEM), lambda bi, i: (bi, i, 0)),
            wspec(wo), wspec(g1), wspec(b1), wspec(wi), wspec(wd), wspec(g2), wspec(b2),
        ],
        out_specs=pl.BlockSpec((1, t, d), lambda bi, i: (bi, i, 0)),
        out_shape=jax.ShapeDtypeStruct((b, s, d), F32),
        scratch_shapes=[pltpu.VMEM((t, d_ff), BF16)],
        compiler_params=pltpu.CompilerParams(dimension_semantics=("parallel", "parallel"),
                                             vmem_limit_bytes=VMEM_LIMIT_BYTES),
        name="post_mixer",
    )(x, mix, mem_out, wo, g1, b1, wi, wd, g2, b2)


def kernel(x, mem, a_w_in, a_ln_v_g, a_ln_v_b, a_w_s, a_b_s, b_w_in, w_mem_kv, w_mix_out,
           ln_mix_g, ln_mix_b, w_ffn_in, w_ffn_out, ln_ffn_g, ln_ffn_b):
    b, s, d = x.shape
    n_mem = mem.shape[1]
    d_ff = w_ffn_out.shape[1]
    assert s % FRONT_TILE == 0 and FRONT_TILE % SUB_TILE == 0 and SUB_TILE % MOBA_BLOCK == 0
    assert s % TOKEN_TILE == 0 and TOKEN_TILE == 2 * SUB_TILE and d_ff % FFN_CHUNK == 0
    n_blk = s // MOBA_BLOCK
    assert n_blk % 2 == 0 and n_blk % SELECT_GROUP == 0

    kv = _mem_kv(mem.reshape(b * n_mem, d), w_mem_kv).reshape(DEPTH, b, n_mem, 2 * W_MEM)

    wo, wi, wd = (w.astype(BF16) for w in (w_mix_out, w_ffn_in, w_ffn_out))
    ln_rows = [v.reshape(DEPTH, 1, d) for v in (ln_mix_g, ln_mix_b, ln_ffn_g, ln_ffn_b)]

    def post(i, xin, mix, mem_out):
        return _post_mixer(i, xin, mix, mem_out, wo, ln_rows[0], ln_rows[1], wi, wd,
                           ln_rows[2], ln_rows[3])

    b_full = jnp.repeat(a_b_s[0].T, HEAD_DIM, axis=1)
    mix, mem_out = _gmlp_front(x, a_w_in[0].astype(BF16), a_ln_v_g[0].reshape(1, -1),
                               a_ln_v_b[0].reshape(1, -1), a_w_s[0].astype(BF16), b_full, kv[0])
    x = post(0, x, mix, mem_out)

    w1 = b_w_in[0].astype(BF16)
    w_qkm = jnp.concatenate([w1[:, :2 * W_MIX], w1[:, 3 * W_MIX:]], axis=1)
    w_vt = w1[:, 2 * W_MIX:3 * W_MIX].T
    q, k, vt, km, mem_out = _moba_front(x, w_qkm, w_vt, kv[1])
    mix = _moba_attn(q, k.reshape(b, n_blk // 2, 2 * MOBA_BLOCK, W_MIX), vt,
                     km.reshape(b, n_blk, W_MIX))
    return post(1, x, mix, mem_out)
```
